```python
import jax, jax.numpy as jnp
from jax import lax
import numpy as np

D_MODEL = 2048
BATCH = 2
SEQ = 8192
DEPTH = 4

N_MEM = 256
HEAD_DIM = 128
DIL_GROUPS = ((128, 1), (512, 4), (2048, 16))
DIL_HEADS = 4
DIL_WIDTH = DIL_HEADS * HEAD_DIM
DIL_TOTAL = len(DIL_GROUPS) * DIL_WIDTH
BLOCK = 128
N_MEM_HEADS = 4
MEM_WIDTH = N_MEM_HEADS * HEAD_DIM
POOL_WINDOWS = (2, 4, 8, 16)
POOL_WIDTH = D_MODEL - MEM_WIDTH
POOL_GROUP = POOL_WIDTH // len(POOL_WINDOWS)
D_FF = D_MODEL * 7 // 2
N_EXPERTS = 8
TOP_K = 2
EPS = 1e-6

kernel_name = 'hybrid_dilated_pool_memory_moe_trunk'


def rmsnorm(x, g):
    xf = x.astype(jnp.float32)
    y = xf * lax.rsqrt(jnp.mean(xf * xf, axis=-1, keepdims=True) + EPS)
    return (y * g.astype(jnp.float32)).astype(x.dtype)


def dilated_window_attention(q, k, v, window, dilation):
    b, s, h, dh = q.shape
    n_back = window // dilation
    L = s // dilation
    nb = -(-L // BLOCK)
    pad = nb * BLOCK - L

    def to_blocks(t):
        t = t.reshape(b, L, dilation, h, dh).transpose(0, 2, 1, 3, 4)
        t = jnp.pad(t, ((0, 0), (0, 0), (0, pad), (0, 0), (0, 0)))
        return t.reshape(b, dilation, nb, BLOCK, h, dh)

    def with_prev(t):
        prev = jnp.pad(t, ((0, 0), (0, 0), (1, 0), (0, 0), (0, 0), (0, 0)))[:, :, :nb]
        return jnp.concatenate([prev, t], axis=3)

    qb = to_blocks(q)
    kb = with_prev(to_blocks(k))
    vb = with_prev(to_blocks(v))
    scores = jnp.einsum('brnqhd,brnkhd->brnhqk', qb, kb).astype(jnp.float32) * (HEAD_DIM ** -0.5)
    qi = jnp.arange(BLOCK)[:, None]
    kj = jnp.arange(2 * BLOCK)[None, :]
    dist = qi + BLOCK - kj
    band = (dist >= 0) & (dist <= n_back)
    has_prev = (jnp.arange(nb) > 0)[:, None, None] | (kj >= BLOCK)[None]
    mask = band[None] & has_prev
    scores = jnp.where(mask[None, None, :, None], scores, -jnp.inf)
    m = jnp.max(scores, axis=-1, keepdims=True)
    p = jnp.exp(scores - m)
    den = jnp.sum(p, axis=-1, keepdims=True)
    out = jnp.einsum('brnhqk,brnkhd->brnqhd', (p / den).astype(v.dtype), vb)
    lse = (m + jnp.log(den))[..., 0]
    out = out.reshape(b, dilation, nb * BLOCK, h, dh)[:, :, :L]
    out = out.transpose(0, 2, 1, 3, 4).reshape(b, s, h, dh)
    lse = lse.transpose(0, 1, 2, 4, 3).reshape(b, dilation, nb * BLOCK, h)[:, :, :L]
    lse = lse.transpose(0, 2, 1, 3).reshape(b, s, h)
    return out, lse


def memory_attention(q_flat, mem, norm_mem, w_mem_kv, q_gain, k_gain):
    b, s, _ = q_flat.shape
    n_mem = mem.shape[1]
    k, v = jnp.split(rmsnorm(mem, norm_mem) @ w_mem_kv, 2, axis=-1)
    q = rmsnorm(q_flat.reshape(b, s, N_MEM_HEADS, HEAD_DIM), q_gain)
    k = rmsnorm(k.reshape(b, n_mem, N_MEM_HEADS, HEAD_DIM), k_gain)
    v = v.reshape(b, n_mem, N_MEM_HEADS, HEAD_DIM)
    scores = jnp.einsum('bshd,bmhd->bhsm', q, k).astype(jnp.float32) * (HEAD_DIM ** -0.5)
    p = jax.nn.softmax(scores, axis=-1)
    out = jnp.einsum('bhsm,bmhd->bshd', p.astype(v.dtype), v)
    return out.reshape(b, s, MEM_WIDTH)


def dilated_mixer_layer(x, mem, norm_mix, norm_mem, w_in, qk_norm, w_mem_kv, w_out):
    b, s, _ = x.shape
    proj = rmsnorm(x, norm_mix) @ w_in
    q_all, k_all, v_all, q_mem = jnp.split(proj, [DIL_TOTAL, 2 * DIL_TOTAL, 3 * DIL_TOTAL], axis=-1)
    outs, lses = [], []
    for g, (window, dilation) in enumerate(DIL_GROUPS):
        cols = slice(g * DIL_WIDTH, (g + 1) * DIL_WIDTH)
        q = rmsnorm(q_all[..., cols].reshape(b, s, DIL_HEADS, HEAD_DIM), qk_norm[0])
        k = rmsnorm(k_all[..., cols].reshape(b, s, DIL_HEADS, HEAD_DIM), qk_norm[1])
        v = v_all[..., cols].reshape(b, s, DIL_HEADS, HEAD_DIM)
        o, lse = dilated_window_attention(q, k, v, window, dilation)
        outs.append(o)
        lses.append(lse)
    outs = jnp.stack(outs, axis=0)
    weights = jax.nn.softmax(jnp.stack(lses, axis=0), axis=0)
    dil_out = jnp.einsum('gbsh,gbshd->bshd', weights, outs.astype(jnp.float32))
    dil_out = dil_out.reshape(b, s, DIL_WIDTH).astype(x.dtype)
    mem_out = memory_attention(q_mem, mem, norm_mem, w_mem_kv, qk_norm[2], qk_norm[3])
    return x + jnp.concatenate([dil_out, mem_out], axis=-1) @ w_out


def causal_pool_residual(u, window):
    s = u.shape[1]
    uf = u.astype(jnp.float32)
    cs = jnp.cumsum(uf, axis=1)
    cs_shift = jnp.pad(cs, ((0, 0), (window, 0), (0, 0)))[:, :s]
    count = jnp.minimum(jnp.arange(1, s + 1), window).astype(jnp.float32)[None, :, None]
    return (cs - cs_shift) / count - uf


def pooling_mixer_layer(x, mem, norm_mix, norm_mem, w_in, w_pool, pool_scale, qk_norm, w_mem_kv, w_out):
    b, s, _ = x.shape
    proj = rmsnorm(x, norm_mix) @ w_in
    u, q_mem = jnp.split(proj, [POOL_WIDTH], axis=-1)
    ug = u.reshape(b, s, len(POOL_WINDOWS), POOL_GROUP)
    pooled = jnp.stack([causal_pool_residual(ug[:, :, g], w) for g, w in enumerate(POOL_WINDOWS)], axis=2)
    pool_out = jnp.einsum('bsgc,gcd->bsgd', pooled.astype(x.dtype), w_pool).reshape(b, s, POOL_WIDTH)
    pool_out = pool_out * pool_scale
    mem_out = memory_attention(q_mem, mem, norm_mem, w_mem_kv, qk_norm[0], qk_norm[1])
    return x + jnp.concatenate([pool_out, mem_out], axis=-1) @ w_out


def swiglu(h, w_gu, w_down):
    gate, up = jnp.split(h @ w_gu, 2, axis=-1)
    return (jax.nn.silu(gate) * up) @ w_down


def moe_swiglu(h, w_router, w_gu_e, w_down_e):
    logits = (h @ w_router).astype(jnp.float32)
    top_val, top_idx = lax.top_k(logits, TOP_K)
    gates = jax.nn.softmax(top_val, axis=-1)
    dense_gate = jnp.sum(jax.nn.one_hot(top_idx, N_EXPERTS, dtype=jnp.float32) * gates[..., None], axis=-2)
    out = jnp.zeros_like(h)
    for e in range(N_EXPERTS):
        out = out + dense_gate[..., e:e + 1].astype(h.dtype) * swiglu(h, w_gu_e[e], w_down_e[e])
    return out


def setup_inputs(seed: int = 0) -> dict:
    key = jax.random.key(seed)
    keys = jax.random.split(key, 64)
    counter = iter(range(64))

    def nk():
        return keys[next(counter)]

    def w(shape, fan_in):
        return jax.random.normal(nk(), shape, jnp.float32) * fan_in ** -0.5

    def gain(shape):
        return 1.0 + 0.05 * jax.random.normal(nk(), shape, jnp.float32)

    inp = {}
    inp['x'] = jax.random.normal(nk(), (BATCH, SEQ, D_MODEL), jnp.float32)
    inp['mem'] = jax.random.normal(nk(), (BATCH, N_MEM, D_MODEL), jnp.float32)
    for i in range(DEPTH):
        p = 'l%d_' % i
        inp[p + 'norm_mix'] = gain((D_MODEL,))
        inp[p + 'norm_mem'] = gain((D_MODEL,))
        if i % 2 == 0:
            inp[p + 'w_in'] = w((D_MODEL, 3 * DIL_TOTAL + MEM_WIDTH), D_MODEL)
            inp[p + 'qk_norm'] = gain((4, HEAD_DIM))
            inp[p + 'w_mem_kv'] = w((D_MODEL, 2 * MEM_WIDTH), D_MODEL)
            inp[p + 'w_out'] = w((DIL_WIDTH + MEM_WIDTH, D_MODEL), DIL_WIDTH + MEM_WIDTH)
            inp[p + 'norm_ffn'] = gain((D_MODEL,))
            inp[p + 'w_gu'] = w((D_MODEL, 2 * D_FF), D_MODEL)
            inp[p + 'w_down'] = w((D_FF, D_MODEL), D_FF)
        else:
            inp[p + 'w_in'] = w((D_MODEL, POOL_WIDTH + MEM_WIDTH), D_MODEL)
            inp[p + 'w_pool'] = w((len(POOL_WINDOWS), POOL_GROUP, POOL_GROUP), POOL_GROUP)
            inp[p + 'pool_scale'] = gain((POOL_WIDTH,))
            inp[p + 'qk_norm'] = gain((2, HEAD_DIM))
            inp[p + 'w_mem_kv'] = w((D_MODEL, 2 * MEM_WIDTH), D_MODEL)
            inp[p + 'w_out'] = w((POOL_WIDTH + MEM_WIDTH, D_MODEL), POOL_WIDTH + MEM_WIDTH)
            inp[p + 'norm_ffn'] = gain((D_MODEL,))
            inp[p + 'w_router'] = w((D_MODEL, N_EXPERTS), D_MODEL)
            inp[p + 'w_gu_e'] = w((N_EXPERTS, D_MODEL, 2 * D_FF), D_MODEL)
            inp[p + 'w_down_e'] = w((N_EXPERTS, D_FF, D_MODEL), D_FF)
    return inp


def reference(x, mem,
              l0_norm_mix, l0_norm_mem, l0_w_in, l0_qk_norm, l0_w_mem_kv, l0_w_out, l0_norm_ffn, l0_w_gu, l0_w_down,
              l1_norm_mix, l1_norm_mem, l1_w_in, l1_w_pool, l1_pool_scale, l1_qk_norm, l1_w_mem_kv, l1_w_out,
              l1_norm_ffn, l1_w_router, l1_w_gu_e, l1_w_down_e,
              l2_norm_mix, l2_norm_mem, l2_w_in, l2_qk_norm, l2_w_mem_kv, l2_w_out, l2_norm_ffn, l2_w_gu, l2_w_down,
              l3_norm_mix, l3_norm_mem, l3_w_in, l3_w_pool, l3_pool_scale, l3_qk_norm, l3_w_mem_kv, l3_w_out,
              l3_norm_ffn, l3_w_router, l3_w_gu_e, l3_w_down_e):
    mixer_params = [
        (l0_norm_mix, l0_norm_mem, l0_w_in, l0_qk_norm, l0_w_mem_kv, l0_w_out),
        (l1_norm_mix, l1_norm_mem, l1_w_in, l1_w_pool, l1_pool_scale, l1_qk_norm, l1_w_mem_kv, l1_w_out),
        (l2_norm_mix, l2_norm_mem, l2_w_in, l2_qk_norm, l2_w_mem_kv, l2_w_out),
        (l3_norm_mix, l3_norm_mem, l3_w_in, l3_w_pool, l3_pool_scale, l3_qk_norm, l3_w_mem_kv, l3_w_out),
    ]
    ffn_norms = [l0_norm_ffn, l1_norm_ffn, l2_norm_ffn, l3_norm_ffn]
    ffn_params = [
        (l0_w_gu, l0_w_down),
        (l1_w_router, l1_w_gu_e, l1_w_down_e),
        (l2_w_gu, l2_w_down),
        (l3_w_router, l3_w_gu_e, l3_w_down_e),
    ]
    for i in range(DEPTH):
        if i % 2 == 0:
            x = dilated_mixer_layer(x, mem, *mixer_params[i])
            x = x + swiglu(rmsnorm(x, ffn_norms[i]), *ffn_params[i])
        else:
            x = pooling_mixer_layer(x, mem, *mixer_params[i])
            x = x + moe_swiglu(rmsnorm(x, ffn_norms[i]), *ffn_params[i])
    return x
```

```python
import functools

import jax
import jax.numpy as jnp
from jax import lax
from jax.experimental import pallas as pl
from jax.experimental.pallas import tpu as pltpu

F32 = jnp.float32
BF16 = jnp.bfloat16
U32 = jnp.uint32
I32 = jnp.int32

D_MODEL = 2048
HEAD_DIM = 128
HEADS_PER_GROUP = 4
GROUP_WIDTH = HEADS_PER_GROUP * HEAD_DIM
DILATIONS = (1, 4, 16)
ATTN_BLOCK = 128
N_MEM = 256
POOL_WINDOWS = (2, 4, 8, 16)
POOL_GROUP = 384
POOL_WIDTH = 4 * POOL_GROUP
POOL_HALO = 16
D_FF = 7168
N_EXPERTS = 8
EPS = 1e-6
SCORE_SCALE = HEAD_DIM ** -0.5

LANES = 128
V7X_VMEM_BYTES = 64 * 1024 * 1024

PROJ_TM, PROJ_TN = 512, 512
ATTN_QB = 512
OUT_TM = 512
FFN_TM, FFN_TF = 512, 512
MOE_TM, MOE_TF = 512, 512
ROUTE_TM = 512
ROW_TM = 256


def _vmem_limit(nbytes):
    return int(min(nbytes * 3 // 2 + (8 << 20), V7X_VMEM_BYTES - (4 << 20)))


def _rms(x, eps=EPS):
    return x * lax.rsqrt(jnp.mean(x * x, axis=-1, keepdims=True) + eps)


def _proj_kernel(x_ref, g_ref, w_ref, hg_ref, o_ref, hn_ref, *, norm_tiles):
    j = pl.program_id(1)

    @pl.when(j == 0)
    def _():
        hn_ref[...] = (_rms(x_ref[...]) * g_ref[...]).astype(BF16)

    acc = jnp.dot(hn_ref[...], w_ref[...], preferred_element_type=F32)
    is_norm = functools.reduce(jnp.logical_or, [j == t for t in norm_tiles])

    @pl.when(is_norm)
    def _():
        for h in range(acc.shape[1] // HEAD_DIM):
            cols = slice(h * HEAD_DIM, (h + 1) * HEAD_DIM)
            o_ref[:, cols] = (_rms(acc[:, cols]) * hg_ref[:, cols]).astype(o_ref.dtype)

    @pl.when(jnp.logical_not(is_norm))
    def _():
        o_ref[...] = acc.astype(o_ref.dtype)


def _proj(x, gain, w_bf16, head_gain, norm_tiles, tm=PROJ_TM, tn=PROJ_TN):
    t, d = x.shape
    n = w_bf16.shape[1]
    est = 2 * tm * d * 4 + tm * d * 2 + 2 * d * tn * 2 + 2 * tm * tn * 2 + 2 * tm * tn * 4
    return pl.pallas_call(
        functools.partial(_proj_kernel, norm_tiles=tuple(norm_tiles)),
        grid=(t // tm, n // tn),
        in_specs=[
            pl.BlockSpec((tm, d), lambda i, j: (i, 0)),
            pl.BlockSpec((1, d), lambda i, j: (0, 0)),
            pl.BlockSpec((d, tn), lambda i, j: (0, j)),
            pl.BlockSpec((1, tn), lambda i, j: (0, j)),
        ],
        out_specs=pl.BlockSpec((tm, tn), lambda i, j: (i, j)),
        out_shape=jax.ShapeDtypeStruct((t, n), BF16),
        scratch_shapes=[pltpu.VMEM((tm, d), BF16)],
        compiler_params=pltpu.CompilerParams(
            dimension_semantics=("arbitrary", "arbitrary"), vmem_limit_bytes=_vmem_limit(est)),
    )(x, gain.reshape(1, d), w_bf16, head_gain)


def _dil_attn_kernel(q_ref, kp_ref, k_ref, vp_ref, v_ref, o_ref, l_ref, *, qb):
    has_prev = pl.program_id(2) > 0
    qi = lax.broadcasted_iota(I32, (ATTN_BLOCK, 2 * ATTN_BLOCK), 0)
    kj = lax.broadcasted_iota(I32, (ATTN_BLOCK, 2 * ATTN_BLOCK), 1)
    in_prev = kj < ATTN_BLOCK
    kk = jnp.where(in_prev, kj, kj - ATTN_BLOCK)
    band = jnp.logical_and(kk >= jnp.where(in_prev, qi, 0), kk <= jnp.where(in_prev, ATTN_BLOCK, qi))
    band_first = jnp.logical_and(band, jnp.logical_or(has_prev, jnp.logical_not(in_prev)))
    for jb in range(qb // ATTN_BLOCK):
        rows = slice(jb * ATTN_BLOCK, (jb + 1) * ATTN_BLOCK)
        for h in range(HEADS_PER_GROUP):
            cols = slice(h * HEAD_DIM, (h + 1) * HEAD_DIM)
            q = q_ref[rows, cols]
            if jb == 0:
                kk = jnp.concatenate([kp_ref[:, cols], k_ref[rows, cols]], axis=0)
                vv = jnp.concatenate([vp_ref[:, cols], v_ref[rows, cols]], axis=0)
                mask = band_first
            else:
                both = slice((jb - 1) * ATTN_BLOCK, (jb + 1) * ATTN_BLOCK)
                kk = k_ref[both, cols]
                vv = v_ref[both, cols]
                mask = band
            s = lax.dot_general(q, kk, (((1,), (1,)), ((), ())), preferred_element_type=F32)
            s = jnp.where(mask, s * SCORE_SCALE, -jnp.inf)
            m = jnp.max(s, axis=-1, keepdims=True)
            p = jnp.exp(s - m)
            den = jnp.sum(p, axis=-1, keepdims=True)
            o = jnp.dot(p.astype(BF16), vv, preferred_element_type=F32)
            o_ref[rows, cols] = o / den
            l_ref[rows, cols] = jnp.broadcast_to(m + jnp.log(den), (ATTN_BLOCK, HEAD_DIM))


def _dil_attn(proj, batch, seq, group, dilation, qb=ATTN_QB):
    n_col = proj.shape[1] // GROUP_WIDTH
    length = seq // dilation
    qb = min(qb, length)
    pv = proj.reshape(batch, length, dilation * proj.shape[1])
    bpq = qb // ATTN_BLOCK

    def cur(col):
        return pl.BlockSpec((None, qb, GROUP_WIDTH), lambda b, r, i: (b, i, r * n_col + col))

    def prev(col):
        return pl.BlockSpec((None, ATTN_BLOCK, GROUP_WIDTH),
                            lambda b, r, i: (b, jnp.maximum(i * bpq - 1, 0), r * n_col + col))

    out_spec = pl.BlockSpec((None, qb, GROUP_WIDTH), lambda b, r, i: (b, i, r))
    out_sds = jax.ShapeDtypeStruct((batch, length, dilation * GROUP_WIDTH), F32)
    est = 2 * (3 * qb + 2 * ATTN_BLOCK) * GROUP_WIDTH * 2 + 2 * 2 * qb * GROUP_WIDTH * 4
    o, l = pl.pallas_call(
        functools.partial(_dil_attn_kernel, qb=qb),
        grid=(batch, dilation, length // qb),
        in_specs=[cur(group), prev(3 + group), cur(3 + group), prev(6 + group), cur(6 + group)],
        out_specs=[out_spec, out_spec],
        out_shape=[out_sds, out_sds],
        compiler_params=pltpu.CompilerParams(
            dimension_semantics=("arbitrary", "arbitrary", "arbitrary"),
            vmem_limit_bytes=_vmem_limit(est)),
    )(pv, pv, pv, pv, pv)
    return o.reshape(batch * seq, GROUP_WIDTH), l.reshape(batch * seq, GROUP_WIDTH)


def _mem_attention(qm_ref, km_ref, vm_ref):
    outs = []
    for h in range(HEADS_PER_GROUP):
        cols = slice(h * HEAD_DIM, (h + 1) * HEAD_DIM)
        s = lax.dot_general(qm_ref[:, cols], km_ref[:, cols], (((1,), (1,)), ((), ())),
                            preferred_element_type=F32) * SCORE_SCALE
        m = jnp.max(s, axis=-1, keepdims=True)
        p = jnp.exp(s - m)
        den = jnp.sum(p, axis=-1, keepdims=True)
        o = jnp.dot(p.astype(BF16), vm_ref[:, cols], preferred_element_type=F32)
        outs.append((o / den).astype(BF16))
    return outs


def _dil_out_kernel(o0_ref, o1_ref, o2_ref, l0_ref, l1_ref, l2_ref, qm_ref, km_ref, vm_ref,
                    x_ref, w_ref, y_ref):
    l0, l1, l2 = l0_ref[...], l1_ref[...], l2_ref[...]
    mx = jnp.maximum(jnp.maximum(l0, l1), l2)
    e0, e1, e2 = jnp.exp(l0 - mx), jnp.exp(l1 - mx), jnp.exp(l2 - mx)
    den = e0 + e1 + e2
    dil = (e0 / den) * o0_ref[...] + (e1 / den) * o1_ref[...] + (e2 / den) * o2_ref[...]
    y = jnp.dot(dil.astype(BF16), w_ref[0:GROUP_WIDTH, :], preferred_element_type=F32)
    for h, mo in enumerate(_mem_attention(qm_ref, km_ref, vm_ref)):
        rows = slice(GROUP_WIDTH + h * HEAD_DIM, GROUP_WIDTH + (h + 1) * HEAD_DIM)
        y = y + jnp.dot(mo, w_ref[rows, :], preferred_element_type=F32)
    y_ref[...] = x_ref[...] + y


def _dil_out(x, outs, lses, proj, kv, w_out_bf16, seq, tm=OUT_TM):
    t, d = x.shape
    tiles_per_seq = seq // tm
    qm_col = proj.shape[1] // GROUP_WIDTH - 1
    grp = pl.BlockSpec((tm, GROUP_WIDTH), lambda i: (i, 0))
    est = (2 * 6 * tm * GROUP_WIDTH * 4 + 2 * tm * GROUP_WIDTH * 2 + 4 * N_MEM * GROUP_WIDTH * 2
           + 4 * tm * d * 4 + 2 * w_out_bf16.size * 2 + 4 * tm * d * 4)
    return pl.pallas_call(
        _dil_out_kernel,
        grid=(t // tm,),
        in_specs=[grp] * 6 + [
            pl.BlockSpec((tm, GROUP_WIDTH), lambda i: (i, qm_col)),
            pl.BlockSpec((None, N_MEM, GROUP_WIDTH), lambda i: (i // tiles_per_seq, 0, 0)),
            pl.BlockSpec((None, N_MEM, GROUP_WIDTH), lambda i: (i // tiles_per_seq, 0, 1)),
            pl.BlockSpec((tm, d), lambda i: (i, 0)),
            pl.BlockSpec(w_out_bf16.shape, lambda i: (0, 0)),
        ],
        out_specs=pl.BlockSpec((tm, d), lambda i: (i, 0)),
        out_shape=jax.ShapeDtypeStruct((t, d), F32),
        compiler_params=pltpu.CompilerParams(
            dimension_semantics=("arbitrary",), vmem_limit_bytes=_vmem_limit(est)),
    )(*outs, *lses, proj, kv, kv, x, w_out_bf16)


def _pool_out_kernel(u_ref, halo_ref, qm_ref, km_ref, vm_ref, x_ref, wp_ref, ps_ref, w_ref, y_ref,
                     *, tiles_per_seq):
    tm = u_ref.shape[0]
    tile_in_seq = pl.program_id(0) % tiles_per_seq
    halo = jnp.where(tile_in_seq > 0, halo_ref[...].astype(F32), 0.0)
    full = jnp.concatenate([halo, u_ref[...].astype(F32)], axis=0)
    pos = tile_in_seq * tm + lax.broadcasted_iota(I32, (tm, 1), 0)
    y = None
    acc = full
    for g, window in enumerate(POOL_WINDOWS):
        lo = g * POOL_GROUP
        acc = acc[:, (POOL_GROUP if g else 0):]
        acc = acc + pltpu.roll(acc, window // 2, 0)
        count = jnp.minimum(pos + 1, window).astype(F32)
        pooled = acc[POOL_HALO:, :POOL_GROUP] / count - full[POOL_HALO:, lo:lo + POOL_GROUP]
        po = jnp.dot(pooled.astype(BF16), wp_ref[g], preferred_element_type=F32)
        po = (po * ps_ref[:, lo:lo + POOL_GROUP]).astype(BF16)
        part = jnp.dot(po, w_ref[lo:lo + POOL_GROUP, :], preferred_element_type=F32)
        y = part if y is None else y + part
    for h, mo in enumerate(_mem_attention(qm_ref, km_ref, vm_ref)):
        rows = slice(POOL_WIDTH + h * HEAD_DIM, POOL_WIDTH + (h + 1) * HEAD_DIM)
        y = y + jnp.dot(mo, w_ref[rows, :], preferred_element_type=F32)
    y_ref[...] = x_ref[...] + y


def _pool_out(x, proj, kv, w_pool_bf16, pool_scale, w_out_bf16, seq, tm=OUT_TM):
    t, d = x.shape
    tiles_per_seq = seq // tm
    halo_blocks = tm // POOL_HALO
    est = (2 * tm * d * 2 + 4 * N_MEM * GROUP_WIDTH * 2 + 4 * tm * d * 4 + 2 * w_out_bf16.size * 2
           + 2 * w_pool_bf16.size * 2 + 8 * tm * POOL_WIDTH * 4)
    return pl.pallas_call(
        functools.partial(_pool_out_kernel, tiles_per_seq=tiles_per_seq),
        grid=(t // tm,),
        in_specs=[
            pl.BlockSpec((tm, POOL_WIDTH), lambda i: (i, 0)),
            pl.BlockSpec((POOL_HALO, POOL_WIDTH), lambda i: (jnp.maximum(i * halo_blocks - 1, 0), 0)),
            pl.BlockSpec((tm, GROUP_WIDTH), lambda i: (i, POOL_WIDTH // GROUP_WIDTH)),
            pl.BlockSpec((None, N_MEM, GROUP_WIDTH), lambda i: (i // tiles_per_seq, 0, 0)),
            pl.BlockSpec((None, N_MEM, GROUP_WIDTH), lambda i: (i // tiles_per_seq, 0, 1)),
            pl.BlockSpec((tm, d), lambda i: (i, 0)),
            pl.BlockSpec(w_pool_bf16.shape, lambda i: (0, 0, 0)),
            pl.BlockSpec((1, POOL_WIDTH), lambda i: (0, 0)),
            pl.BlockSpec(w_out_bf16.shape, lambda i: (0, 0)),
        ],
        out_specs=pl.BlockSpec((tm, d), lambda i: (i, 0)),
        out_shape=jax.ShapeDtypeStruct((t, d), F32),
        compiler_params=pltpu.CompilerParams(
            dimension_semantics=("arbitrary",), vmem_limit_bytes=_vmem_limit(est)),
    )(proj, proj, proj, kv, kv, x, w_pool_bf16, pool_scale.reshape(1, POOL_WIDTH), w_out_bf16)


def _swiglu_step(hn, wg_ref, wu_ref, wd_ref):
    gate = jnp.dot(hn, wg_ref[...], preferred_element_type=F32)
    up = jnp.dot(hn, wu_ref[...], preferred_element_type=F32)
    act = (gate * (1.0 / (1.0 + jnp.exp(-gate))) * up).astype(BF16)
    return jnp.dot(act, wd_ref[...], preferred_element_type=F32)


def _ffn_dense_kernel(x_ref, g_ref, wg_ref, wu_ref, wd_ref, y_ref, hn_ref):
    @pl.when(pl.program_id(1) == 0)
    def _():
        x = x_ref[...]
        hn_ref[...] = (_rms(x) * g_ref[...]).astype(BF16)
        y_ref[...] = x

    y_ref[...] += _swiglu_step(hn_ref[...], wg_ref, wu_ref, wd_ref)


def _ffn_dense(x, gain, w_gu_bf16, w_down_bf16, tm=FFN_TM, tf=FFN_TF):
    t, d = x.shape
    nf = D_FF // tf
    est = 4 * tm * d * 4 + tm * d * 2 + 2 * 3 * d * tf * 2 + 3 * tm * tf * 4 + tm * d * 4
    return pl.pallas_call(
        _ffn_dense_kernel,
        grid=(t // tm, nf),
        in_specs=[
            pl.BlockSpec((tm, d), lambda i, f: (i, 0)),
            pl.BlockSpec((1, d), lambda i, f: (0, 0)),
            pl.BlockSpec((d, tf), lambda i, f: (0, f)),
            pl.BlockSpec((d, tf), lambda i, f: (0, f + nf)),
            pl.BlockSpec((tf, d), lambda i, f: (f, 0)),
        ],
        out_specs=pl.BlockSpec((tm, d), lambda i, f: (i, 0)),
        out_shape=jax.ShapeDtypeStruct((t, d), F32),
        scratch_shapes=[pltpu.VMEM((tm, d), BF16)],
        compiler_params=pltpu.CompilerParams(
            dimension_semantics=("arbitrary", "arbitrary"), vmem_limit_bytes=_vmem_limit(est)),
    )(x, gain.reshape(1, d), w_gu_bf16, w_gu_bf16, w_down_bf16)


def _route_kernel(x_ref, g_ref, whi_ref, wlo_ref, tri_ref, hp_ref, info_ref, cnt_ref, carry_ref):
    i = pl.program_id(0)

    @pl.when(i == 0)
    def _():
        carry_ref[...] = jnp.zeros_like(carry_ref)

    hn = _rms(x_ref[...]) * g_ref[...]
    hi = hn.astype(BF16)
    hi32 = hi.astype(F32)
    lo = (hn - hi32).astype(BF16)
    logits = (jnp.dot(hi, whi_ref[...], preferred_element_type=F32)
              + jnp.dot(hi, wlo_ref[...], preferred_element_type=F32)
              + jnp.dot(lo, whi_ref[...], preferred_element_type=F32))
    tm = logits.shape[0]
    lane = lax.broadcasted_iota(I32, (tm, LANES), 1).astype(F32)
    lg = jnp.where(lane < N_EXPERTS, logits, -jnp.inf)
    m1 = jnp.max(lg, axis=-1, keepdims=True)
    i1 = jnp.min(jnp.where(lg == m1, lane, float(LANES)), axis=-1, keepdims=True)
    lg2 = jnp.where(lane == i1, -jnp.inf, lg)
    m2 = jnp.max(lg2, axis=-1, keepdims=True)
    i2 = jnp.min(jnp.where(lg2 == m2, lane, float(LANES)), axis=-1, keepdims=True)
    e = jnp.exp(m2 - m1)
    g1 = 1.0 / (1.0 + e)
    g2 = e / (1.0 + e)
    sel1 = lane == i1
    sel2 = lane == i2
    chosen = jnp.logical_or(sel1, sel2)
    onehot = jnp.where(chosen, 1.0, 0.0)
    before = jnp.dot(tri_ref[...], onehot.astype(BF16), preferred_element_type=F32) + carry_ref[0:1, :]
    r1 = jnp.sum(jnp.where(sel1, before, 0.0), axis=-1, keepdims=True)
    r2 = jnp.sum(jnp.where(sel2, before, 0.0), axis=-1, keepdims=True)
    carry_ref[0:1, :] = carry_ref[0:1, :] + jnp.sum(onehot, axis=0, keepdims=True)
    cnt_ref[...] = carry_ref[...]

    info = jnp.where(lane == 0, i1, 0.0)
    info = jnp.where(lane == 1, i2, info)
    info = jnp.where(lane == 2, r1, info)
    info = jnp.where(lane == 3, r2, info)
    info = jnp.where(lane == 4, g1, info)
    info = jnp.where(lane == 5, g2, info)
    info_ref[...] = info

    bits = pltpu.bitcast(hi32, U32)
    half = bits.shape[1] // 2
    hp_ref[...] = (bits[:, :half] & jnp.uint32(0xFFFF0000)) | (bits[:, half:] >> 16)


def _route(x, gain, w_router, tm=ROUTE_TM):
    t, d = x.shape
    wpad = jnp.zeros((d, LANES), F32).at[:, :N_EXPERTS].set(w_router)
    whi = wpad.astype(BF16)
    wlo = (wpad - whi.astype(F32)).astype(BF16)
    tri = (lax.broadcasted_iota(I32, (tm, tm), 0) > lax.broadcasted_iota(I32, (tm, tm), 1)).astype(BF16)
    est = 4 * tm * d * 4 + 2 * tm * d * 2 + 4 * d * LANES * 2 + 2 * tm * tm * 2 + 8 * tm * LANES * 4
    return pl.pallas_call(
        _route_kernel,
        grid=(t // tm,),
        in_specs=[
            pl.BlockSpec((tm, d), lambda i: (i, 0)),
            pl.BlockSpec((1, d), lambda i: (0, 0)),
            pl.BlockSpec((d, LANES), lambda i: (0, 0)),
            pl.BlockSpec((d, LANES), lambda i: (0, 0)),
            pl.BlockSpec((tm, tm), lambda i: (0, 0)),
        ],
        out_specs=[
            pl.BlockSpec((tm, d // 2), lambda i: (i, 0)),
            pl.BlockSpec((tm, LANES), lambda i: (i, 0)),
            pl.BlockSpec((8, LANES), lambda i: (0, 0)),
        ],
        out_shape=[
            jax.ShapeDtypeStruct((t, d // 2), U32),
            jax.ShapeDtypeStruct((t, LANES), F32),
            jax.ShapeDtypeStruct((8, LANES), F32),
        ],
        scratch_shapes=[pltpu.VMEM((8, LANES), F32)],
        compiler_params=pltpu.CompilerParams(
            dimension_semantics=("arbitrary",), vmem_limit_bytes=_vmem_limit(est)),
    )(x, gain.reshape(1, d), whi, wlo, tri)


def _scatter_kernel(pos_ref, hp_ref, xs_hbm_in, xs_hbm, sem):
    del xs_hbm_in
    tm = hp_ref.shape[0]

    def row_copy(r, k):
        return pltpu.make_async_copy(hp_ref.at[pl.ds(r, 1)], xs_hbm.at[pl.ds(pos_ref[0, 2 * r + k], 1)], sem)

    def start(r, c):
        row_copy(r, 0).start()
        row_copy(r, 1).start()
        return c

    def wait(r, c):
        row_copy(r, 0).wait()
        row_copy(r, 1).wait()
        return c

    lax.fori_loop(0, tm, start, 0)
    lax.fori_loop(0, tm, wait, 0)


def _scatter_rows(hp, pos, n_slots, tm=ROW_TM):
    t, w = hp.shape
    pos3 = pos.reshape(t // tm, 1, 2 * tm)
    return pl.pallas_call(
        _scatter_kernel,
        grid=(t // tm,),
        in_specs=[
            pl.BlockSpec((None, 1, 2 * tm), lambda i: (i, 0, 0), memory_space=pltpu.SMEM),
            pl.BlockSpec((tm, w), lambda i: (i, 0)),
            pl.BlockSpec(memory_space=pl.ANY),
        ],
        out_specs=pl.BlockSpec(memory_space=pl.ANY),
        out_shape=jax.ShapeDtypeStruct((n_slots, w), U32),
        scratch_shapes=[pltpu.SemaphoreType.DMA(())],
        input_output_aliases={2: 0},
        compiler_params=pltpu.CompilerParams(dimension_semantics=("arbitrary",)),
    )(pos3, hp, jnp.zeros((n_slots, w), U32))


def _combine_kernel(pos_ref, x_ref, info_ref, ys_hbm, y_ref, buf_ref, sem):
    tm = x_ref.shape[0]

    def row_copy(r, k):
        return pltpu.make_async_copy(ys_hbm.at[pl.ds(pos_ref[0, 2 * r + k], 1)],
                                     buf_ref.at[k, pl.ds(r, 1)], sem)

    def start(r, c):
        row_copy(r, 0).start()
        row_copy(r, 1).start()
        return c

    def wait(r, c):
        row_copy(r, 0).wait()
        row_copy(r, 1).wait()
        return c

    lax.fori_loop(0, tm, start, 0)
    lax.fori_loop(0, tm, wait, 0)
    info = info_ref[...]
    y_ref[...] = x_ref[...] + info[:, 4:5] * buf_ref[0] + info[:, 5:6] * buf_ref[1]


def _combine(x, info, ys, pos, tm=ROW_TM):
    t, d = x.shape
    pos3 = pos.reshape(t // tm, 1, 2 * tm)
    est = 4 * tm * d * 4 + 2 * tm * LANES * 4 + 2 * tm * d * 4 + 2 * tm * d * 4
    return pl.pallas_call(
        _combine_kernel,
        grid=(t // tm,),
        in_specs=[
            pl.BlockSpec((None, 1, 2 * tm), lambda i: (i, 0, 0), memory_space=pltpu.SMEM),
            pl.BlockSpec((tm, d), lambda i: (i, 0)),
            pl.BlockSpec((tm, LANES), lambda i: (i, 0)),
            pl.BlockSpec(memory_space=pl.ANY),
        ],
        out_specs=pl.BlockSpec((tm, d), lambda i: (i, 0)),
        out_shape=jax.ShapeDtypeStruct((t, d), F32),
        scratch_shapes=[pltpu.VMEM((2, tm, d), F32), pltpu.SemaphoreType.DMA(())],
        compiler_params=pltpu.CompilerParams(
            dimension_semantics=("arbitrary",), vmem_limit_bytes=_vmem_limit(est)),
    )(pos3, x, info, ys)


def _ffn_moe_kernel(te_ref, tr_ref, xs_ref, wg_ref, wu_ref, wd_ref, y_ref, hn_ref):
    del te_ref
    i, f = pl.program_id(0), pl.program_id(1)
    used = tr_ref[i] > 0

    @pl.when(f == 0)
    def _():
        y_ref[...] = jnp.zeros_like(y_ref)

    @pl.when(jnp.logical_and(used, f == 0))
    def _():
        words = xs_ref[...]
        half = words.shape[1]
        hn_ref[:, :half] = pltpu.bitcast(words & jnp.uint32(0xFFFF0000), F32).astype(BF16)
        hn_ref[:, half:] = pltpu.bitcast(words << 16, F32).astype(BF16)

    @pl.when(used)
    def _():
        y_ref[...] += _swiglu_step(hn_ref[...], wg_ref, wu_ref, wd_ref)


def _ffn_moe(xs, tile_expert, tile_rows, w_gu_bf16, w_down_bf16, tm=MOE_TM, tf=MOE_TF):
    n_slots, half = xs.shape
    d = 2 * half
    nf = D_FF // tf
    n_tiles = n_slots // tm

    def f_eff(i, f, tr):
        return jnp.where(tr[i] > 0, f, nf - 1)

    est = 2 * tm * half * 4 + 2 * tm * d * 4 + tm * d * 2 + 2 * 3 * d * tf * 2 + 3 * tm * tf * 4 + tm * d * 4
    return pl.pallas_call(
        _ffn_moe_kernel,
        grid_spec=pltpu.PrefetchScalarGridSpec(
            num_scalar_prefetch=2,
            grid=(n_tiles, nf),
            in_specs=[
                pl.BlockSpec((tm, half), lambda i, f, te, tr: (i, 0)),
                pl.BlockSpec((None, d, tf), lambda i, f, te, tr: (te[i], 0, f_eff(i, f, tr))),
                pl.BlockSpec((None, d, tf), lambda i, f, te, tr: (te[i], 0, f_eff(i, f, tr) + nf)),
                pl.BlockSpec((None, tf, d), lambda i, f, te, tr: (te[i], f_eff(i, f, tr), 0)),
            ],
            out_specs=pl.BlockSpec((tm, d), lambda i, f, te, tr: (i, 0)),
            scratch_shapes=[pltpu.VMEM((tm, d), BF16)],
        ),
        out_shape=jax.ShapeDtypeStruct((n_slots, d), F32),
        compiler_params=pltpu.CompilerParams(
            dimension_semantics=("arbitrary", "arbitrary"), vmem_limit_bytes=_vmem_limit(est)),
    )(tile_expert, tile_rows, xs, w_gu_bf16, w_gu_bf16, w_down_bf16)


def _moe_layer(x, gain, w_router, w_gu_bf16, w_down_bf16, tm=MOE_TM):
    t, _ = x.shape
    hp, info, cnt = _route(x, gain, w_router)
    counts = cnt[0, :N_EXPERTS].astype(I32)
    tiles_per_expert = (counts + tm - 1) // tm
    tile_end = jnp.cumsum(tiles_per_expert)
    group_start = (tile_end - tiles_per_expert) * tm
    n_tiles = (2 * t + N_EXPERTS * (tm - 1)) // tm
    tile_ids = jnp.arange(n_tiles, dtype=I32)
    tile_expert = jnp.minimum(jnp.sum(tile_ids[:, None] >= tile_end[None, :], axis=1), N_EXPERTS - 1).astype(I32)
    last_used_expert = tile_expert[jnp.maximum(tile_end[-1] - 1, 0)]
    used = tile_ids < tile_end[-1]
    tile_expert = jnp.where(used, tile_expert, last_used_expert)
    rows_left = counts[tile_expert] - (tile_ids * tm - group_start[tile_expert])
    tile_rows = jnp.where(used, jnp.clip(rows_left, 0, tm), 0).astype(I32)
    idx = info[:, 0:2].astype(I32)
    rank = info[:, 2:4].astype(I32)
    pos = (group_start[idx] + rank).astype(I32)

    xs = _scatter_rows(hp, pos, n_tiles * tm)
    ys = _ffn_moe(xs, tile_expert, tile_rows, w_gu_bf16, w_down_bf16)
    return _combine(x, info, ys, pos)


def _mem_kv(mem2d, norm_mem, w_mem_kv, k_gain):
    head_gain = jnp.concatenate([jnp.tile(k_gain, HEADS_PER_GROUP), jnp.ones((GROUP_WIDTH,), F32)])[None, :]
    kv = _proj(mem2d, norm_mem, w_mem_kv.astype(BF16), head_gain, norm_tiles=(0,),
               tm=mem2d.shape[0], tn=GROUP_WIDTH)
    return kv.reshape(-1, N_MEM, 2 * GROUP_WIDTH)


def _dilated_mixer(x, mem2d, batch, seq, norm_mix, norm_mem, w_in, qk_norm, w_mem_kv, w_out):
    n_grp = len(DILATIONS)
    head_gain = jnp.concatenate([
        jnp.tile(qk_norm[0], n_grp * HEADS_PER_GROUP), jnp.tile(qk_norm[1], n_grp * HEADS_PER_GROUP),
        jnp.ones((n_grp * GROUP_WIDTH,), F32), jnp.tile(qk_norm[2], HEADS_PER_GROUP)])[None, :]
    norm_tiles = tuple(range(2 * n_grp)) + (3 * n_grp,)
    proj = _proj(x, norm_mix, w_in.astype(BF16), head_gain, norm_tiles)
    kv = _mem_kv(mem2d, norm_mem, w_mem_kv, qk_norm[3])
    outs, lses = [], []
    for g, dilation in enumerate(DILATIONS):
        o, l = _dil_attn(proj, batch, seq, g, dilation)
        outs.append(o)
        lses.append(l)
    return _dil_out(x, outs, lses, proj, kv, w_out.astype(BF16), seq)


def _pooling_mixer(x, mem2d, batch, seq, norm_mix, norm_mem, w_in, w_pool, pool_scale, qk_norm, w_mem_kv, w_out):
    head_gain = jnp.concatenate([jnp.ones((POOL_WIDTH,), F32), jnp.tile(qk_norm[0], HEADS_PER_GROUP)])[None, :]
    proj = _proj(x, norm_mix, w_in.astype(BF16), head_gain, norm_tiles=(POOL_WIDTH // GROUP_WIDTH,))
    kv = _mem_kv(mem2d, norm_mem, w_mem_kv, qk_norm[1])
    return _pool_out(x, proj, kv, w_pool.astype(BF16), pool_scale, w_out.astype(BF16), seq)


def kernel(x, mem, l0_norm_mix, l0_norm_mem, l0_w_in, l0_qk_norm, l0_w_mem_kv, l0_w_out, l0_norm_ffn, l0_w_gu, l0_w_down, l1_norm_mix, l1_norm_mem, l1_w_in, l1_w_pool, l1_pool_scale, l1_qk_norm, l1_w_mem_kv, l1_w_out, l1_norm_ffn, l1_w_router, l1_w_gu_e, l1_w_down_e, l2_norm_mix, l2_norm_mem, l2_w_in, l2_qk_norm, l2_w_mem_kv, l2_w_out, l2_norm_ffn, l2_w_gu, l2_w_down, l3_norm_mix, l3_norm_mem, l3_w_in, l3_w_pool, l3_pool_scale, l3_qk_norm, l3_w_mem_kv, l3_w_out, l3_norm_ffn, l3_w_router, l3_w_gu_e, l3_w_down_e):
    batch, seq, d = x.shape
    h = x.reshape(batch * seq, d)
    mem2d = mem.reshape(-1, d)

    h = _dilated_mixer(h, mem2d, batch, seq, l0_norm_mix, l0_norm_mem, l0_w_in, l0_qk_norm, l0_w_mem_kv, l0_w_out)
    h = _ffn_dense(h, l0_norm_ffn, l0_w_gu.astype(BF16), l0_w_down.astype(BF16))
    h = _pooling_mixer(h, mem2d, batch, seq, l1_norm_mix, l1_norm_mem, l1_w_in, l1_w_pool, l1_pool_scale,
                       l1_qk_norm, l1_w_mem_kv, l1_w_out)
    h = _moe_layer(h, l1_norm_ffn, l1_w_router, l1_w_gu_e.astype(BF16), l1_w_down_e.astype(BF16))
    h = _dilated_mixer(h, mem2d, batch, seq, l2_norm_mix, l2_norm_mem, l2_w_in, l2_qk_norm, l2_w_mem_kv, l2_w_out)
    h = _ffn_dense(h, l2_norm_ffn, l2_w_gu.astype(BF16), l2_w_down.astype(BF16))
    h = _pooling_mixer(h, mem2d, batch, seq, l3_norm_mix, l3_norm_mem, l3_w_in, l3_w_pool, l3_pool_scale,
                       l3_qk_norm, l3_w_mem_kv, l3_w_out)
    h = _moe_layer(h, l3_norm_ffn, l3_w_router, l3_w_gu_e.astype(BF16), l3_w_down_e.astype(BF16))
    return h.reshape(batch, seq, d)
```

```python
import functools

import jax
import jax.numpy as jnp
from jax import lax
from jax.experimental import pallas as pl
from jax.experimental.pallas import tpu as pltpu

F32 = jnp.float32
BF16 = jnp.bfloat16
U32 = jnp.uint32
I32 = jnp.int32

HEAD_DIM = 128
HEADS_PER_GROUP = 4
GROUP_WIDTH = HEADS_PER_GROUP * HEAD_DIM
DILATIONS = (1, 4, 16)
ATTN_BLOCK = 128
N_MEM = 256
POOL_WINDOWS = (2, 4, 8, 16)
POOL_GROUP = 384
POOL_WIDTH = 4 * POOL_GROUP
POOL_HALO = 16
D_FF = 7168
N_EXPERTS = 8
EPS = 1e-6
SCORE_SCALE = HEAD_DIM ** -0.5

LANES = 128
SUBLANES = 8
V7X_VMEM_BYTES = 64 * 1024 * 1024

PROJ_TM = 512
ATTN_QB = 512
OUT_TM = 512
FFN_TM, FFN_TF = 1024, 512
ROUTE_TM = 512
ROW_TM = 256
ROW_UNROLL = 8


def _vmem_limit(nbytes):
    return int(min(nbytes * 5 // 4 + (6 << 20), V7X_VMEM_BYTES - (4 << 20)))


def _rms(x, eps=EPS):
    return x * lax.rsqrt(jnp.mean(x * x, axis=-1, keepdims=True) + eps)


def _resident(shape):
    return pl.BlockSpec(shape, lambda *_: (0,) * len(shape), pipeline_mode=pl.Buffered(1))


def _proj_kernel(x_ref, g_ref, w_ref, hg_ref, *refs, segments):
    outs, (hn_ref, stage_ref) = refs[:len(segments)], refs[len(segments):]
    tm = x_ref.shape[0]
    hn_ref[...] = (_rms(x_ref[...]) * g_ref[...]).astype(BF16)
    for (col0, width, norm, dilation), o_ref in zip(segments, outs):
        for sub in range(width // GROUP_WIDTH):
            c0 = col0 + sub * GROUP_WIDTH
            acc = jnp.dot(hn_ref[...], w_ref[:, c0:c0 + GROUP_WIDTH], preferred_element_type=F32)
            for h in range(HEADS_PER_GROUP):
                cols = slice(h * HEAD_DIM, (h + 1) * HEAD_DIM)
                val = acc[:, cols]
                if norm:
                    val = _rms(val) * hg_ref[:, c0 + h * HEAD_DIM:c0 + (h + 1) * HEAD_DIM]
                if dilation == 1:
                    o_ref[:, sub * GROUP_WIDTH + h * HEAD_DIM:sub * GROUP_WIDTH + (h + 1) * HEAD_DIM] = val.astype(BF16)
                else:
                    stage_ref[h] = val
                    for r in range(dilation):
                        o_ref[r, :, cols] = stage_ref[h, pl.ds(r, tm // dilation, stride=dilation), :].astype(BF16)


def _proj(x, gain, w_bf16, head_gain, segments, seq, tm=PROJ_TM):
    t, d = x.shape
    tm = min(tm, t)
    tiles_per_seq = max(seq // tm, 1)
    out_specs, out_shapes = [], []
    for _, width, _, dilation in segments:
        if dilation == 1:
            out_specs.append(pl.BlockSpec((tm, width), lambda i: (i, 0)))
            out_shapes.append(jax.ShapeDtypeStruct((t, width), BF16))
        else:
            out_specs.append(pl.BlockSpec((None, dilation, tm // dilation, width),
                                          lambda i: (i // tiles_per_seq, 0, i % tiles_per_seq, 0)))
            out_shapes.append(jax.ShapeDtypeStruct((t // seq, dilation, seq // dilation, width), BF16))
    n_out_cols = sum(s[1] for s in segments)
    est = (2 * tm * d * 4 + tm * d * 2 + w_bf16.size * 2 + 4 * tm * n_out_cols + tm * GROUP_WIDTH * 4
           + 4 * tm * GROUP_WIDTH * 4)
    return pl.pallas_call(
        functools.partial(_proj_kernel, segments=tuple(segments)),
        grid=(t // tm,),
        in_specs=[
            pl.BlockSpec((tm, d), lambda i: (i, 0)),
            _resident((1, d)),
            _resident(w_bf16.shape),
            _resident(head_gain.shape),
        ],
        out_specs=out_specs,
        out_shape=out_shapes,
        scratch_shapes=[pltpu.VMEM((tm, d), BF16), pltpu.VMEM((HEADS_PER_GROUP, tm, HEAD_DIM), F32)],
        compiler_params=pltpu.CompilerParams(
            dimension_semantics=("arbitrary",), vmem_limit_bytes=_vmem_limit(est)),
        name="proj",
    )(x, gain.reshape(1, d), w_bf16, head_gain)


def _dil_attn_kernel(q_ref, kp_ref, k_ref, vp_ref, v_ref, o_ref, l_ref, *, qb):
    has_prev = pl.program_id(2) > 0
    qi = lax.broadcasted_iota(I32, (ATTN_BLOCK, 2 * ATTN_BLOCK), 0)
    kj = lax.broadcasted_iota(I32, (ATTN_BLOCK, 2 * ATTN_BLOCK), 1)
    in_prev = kj < ATTN_BLOCK
    kk = jnp.where(in_prev, kj, kj - ATTN_BLOCK)
    band = jnp.logical_and(kk >= jnp.where(in_prev, qi, 0), kk <= jnp.where(in_prev, ATTN_BLOCK, qi))
    band_first = jnp.logical_and(band, jnp.logical_or(has_prev, jnp.logical_not(in_prev)))
    for jb in range(qb // ATTN_BLOCK):
        rows = slice(jb * ATTN_BLOCK, (jb + 1) * ATTN_BLOCK)
        for h in range(HEADS_PER_GROUP):
            cols = slice(h * HEAD_DIM, (h + 1) * HEAD_DIM)
            q = q_ref[rows, cols]
            if jb == 0:
                keys = jnp.concatenate([kp_ref[:, cols], k_ref[rows, cols]], axis=0)
                vals = jnp.concatenate([vp_ref[:, cols], v_ref[rows, cols]], axis=0)
                mask = band_first
            else:
                both = slice((jb - 1) * ATTN_BLOCK, (jb + 1) * ATTN_BLOCK)
                keys = k_ref[both, cols]
                vals = v_ref[both, cols]
                mask = band
            s = lax.dot_general(q, keys, (((1,), (1,)), ((), ())), preferred_element_type=F32)
            s = jnp.where(mask, s * SCORE_SCALE, -jnp.inf)
            m = jnp.max(s, axis=-1, keepdims=True)
            p = jnp.exp(s - m)
            den = jnp.sum(p, axis=-1, keepdims=True)
            o = jnp.dot(p.astype(BF16), vals, preferred_element_type=F32)
            o_ref[rows, cols] = o / den
            l_ref[rows, cols] = jnp.broadcast_to(m + jnp.log(den), (ATTN_BLOCK, HEAD_DIM))


def _dil_attn(q, k, v, qb=ATTN_QB):
    batch, dilation, length, width = q.shape
    qb = min(qb, length)
    bpq = qb // ATTN_BLOCK
    cur = pl.BlockSpec((None, None, qb, width), lambda b, r, i: (b, r, i, 0))
    prev = pl.BlockSpec((None, None, ATTN_BLOCK, width), lambda b, r, i: (b, r, jnp.maximum(i * bpq - 1, 0), 0))
    out_sds = jax.ShapeDtypeStruct(q.shape, F32)
    est = 2 * (3 * qb + 2 * ATTN_BLOCK) * width * 2 + 2 * 2 * qb * width * 4
    return pl.pallas_call(
        functools.partial(_dil_attn_kernel, qb=qb),
        grid=(batch, dilation, length // qb),
        in_specs=[cur, prev, cur, prev, cur],
        out_specs=[cur, cur],
        out_shape=[out_sds, out_sds],
        compiler_params=pltpu.CompilerParams(
            dimension_semantics=("arbitrary", "arbitrary", "arbitrary"),
            vmem_limit_bytes=_vmem_limit(est)),
        name="dil_attn",
    )(q, k, k, v, v)


def _mem_attention(qm_ref, km_ref, vm_ref):
    outs = []
    for h in range(HEADS_PER_GROUP):
        cols = slice(h * HEAD_DIM, (h + 1) * HEAD_DIM)
        s = lax.dot_general(qm_ref[:, cols], km_ref[:, cols], (((1,), (1,)), ((), ())),
                            preferred_element_type=F32) * SCORE_SCALE
        m = jnp.max(s, axis=-1, keepdims=True)
        p = jnp.exp(s - m)
        den = jnp.sum(p, axis=-1, keepdims=True)
        o = jnp.dot(p.astype(BF16), vm_ref[:, cols], preferred_element_type=F32)
        outs.append((o / den).astype(BF16))
    return outs


def _to_token_order(src_ref, dst_ref):
    dilation, per_res, _ = src_ref.shape
    for h in range(HEADS_PER_GROUP):
        for r in range(dilation):
            dst_ref[h, pl.ds(r, per_res, stride=dilation), :] = src_ref[r, :, h * HEAD_DIM:(h + 1) * HEAD_DIM]


def _dil_out_kernel(o0_ref, l0_ref, o1_ref, l1_ref, o2_ref, l2_ref, qm_ref, km_ref, vm_ref,
                    x_ref, w_ref, y_ref, o1s, l1s, o2s, l2s, dil_ref):
    for src, dst in ((o1_ref, o1s), (l1_ref, l1s), (o2_ref, o2s), (l2_ref, l2s)):
        _to_token_order(src, dst)
    for h in range(HEADS_PER_GROUP):
        cols = slice(h * HEAD_DIM, (h + 1) * HEAD_DIM)
        l0, l1, l2 = l0_ref[:, cols], l1s[h], l2s[h]
        mx = jnp.maximum(jnp.maximum(l0, l1), l2)
        e0, e1, e2 = jnp.exp(l0 - mx), jnp.exp(l1 - mx), jnp.exp(l2 - mx)
        den = e0 + e1 + e2
        dil = (e0 / den) * o0_ref[:, cols] + (e1 / den) * o1s[h] + (e2 / den) * o2s[h]
        dil_ref[:, cols] = dil.astype(BF16)
    y = jnp.dot(dil_ref[...], w_ref[0:GROUP_WIDTH, :], preferred_element_type=F32)
    for h, mo in enumerate(_mem_attention(qm_ref, km_ref, vm_ref)):
        rows = slice(GROUP_WIDTH + h * HEAD_DIM, GROUP_WIDTH + (h + 1) * HEAD_DIM)
        y = y + jnp.dot(mo, w_ref[rows, :], preferred_element_type=F32)
    y_ref[...] = x_ref[...] + y


def _dil_out(x, attn, qm, km, vm, w_out_bf16, seq, tm=OUT_TM):
    t, d = x.shape
    tiles_per_seq = seq // tm

    def grp(dilation):
        return pl.BlockSpec((None, dilation, tm // dilation, GROUP_WIDTH),
                            lambda i: (i // tiles_per_seq, 0, i % tiles_per_seq, 0))

    tok = pl.BlockSpec((tm, GROUP_WIDTH), lambda i: (i, 0))
    mem = pl.BlockSpec((None, N_MEM, GROUP_WIDTH), lambda i: (i // tiles_per_seq, 0, 0))
    (o0, l0), (o1, l1), (o2, l2) = attn
    stage = pltpu.VMEM((HEADS_PER_GROUP, tm, HEAD_DIM), F32)
    est = (2 * 6 * tm * GROUP_WIDTH * 4 + 4 * tm * GROUP_WIDTH * 4 + 2 * tm * GROUP_WIDTH * 2
           + 4 * N_MEM * GROUP_WIDTH * 2 + 4 * tm * d * 4 + w_out_bf16.size * 2 + 3 * tm * d * 4)
    return pl.pallas_call(
        _dil_out_kernel,
        grid=(t // tm,),
        in_specs=[tok, tok, grp(DILATIONS[1]), grp(DILATIONS[1]), grp(DILATIONS[2]), grp(DILATIONS[2]),
                  tok, mem, mem, pl.BlockSpec((tm, d), lambda i: (i, 0)), _resident(w_out_bf16.shape)],
        out_specs=pl.BlockSpec((tm, d), lambda i: (i, 0)),
        out_shape=jax.ShapeDtypeStruct((t, d), F32),
        scratch_shapes=[stage, stage, stage, stage, pltpu.VMEM((tm, GROUP_WIDTH), BF16)],
        compiler_params=pltpu.CompilerParams(
            dimension_semantics=("arbitrary",), vmem_limit_bytes=_vmem_limit(est)),
        name="dil_out",
    )(o0.reshape(t, GROUP_WIDTH), l0.reshape(t, GROUP_WIDTH), o1, l1, o2, l2, qm, km, vm, x, w_out_bf16)


def _pool_out_kernel(u_ref, halo_ref, qm_ref, km_ref, vm_ref, x_ref, wp_ref, ps_ref, w_ref, y_ref,
                     *, tiles_per_seq):
    tm = u_ref.shape[0]
    tile_in_seq = pl.program_id(0) % tiles_per_seq
    halo = jnp.where(tile_in_seq > 0, halo_ref[...].astype(F32), 0.0)
    full = jnp.concatenate([halo, u_ref[...].astype(F32)], axis=0)
    pos = tile_in_seq * tm + lax.broadcasted_iota(I32, (tm, 1), 0)
    y = None
    acc = full
    for g, window in enumerate(POOL_WINDOWS):
        lo = g * POOL_GROUP
        acc = acc[:, (POOL_GROUP if g else 0):]
        acc = acc + pltpu.roll(acc, window // 2, 0)
        count = jnp.minimum(pos + 1, window).astype(F32)
        pooled = acc[POOL_HALO:, :POOL_GROUP] / count - full[POOL_HALO:, lo:lo + POOL_GROUP]
        po = jnp.dot(pooled.astype(BF16), wp_ref[g], preferred_element_type=F32)
        po = (po * ps_ref[:, lo:lo + POOL_GROUP]).astype(BF16)
        part = jnp.dot(po, w_ref[lo:lo + POOL_GROUP, :], preferred_element_type=F32)
        y = part if y is None else y + part
    for h, mo in enumerate(_mem_attention(qm_ref, km_ref, vm_ref)):
        rows = slice(POOL_WIDTH + h * HEAD_DIM, POOL_WIDTH + (h + 1) * HEAD_DIM)
        y = y + jnp.dot(mo, w_ref[rows, :], preferred_element_type=F32)
    y_ref[...] = x_ref[...] + y


def _pool_out(x, u, qm, km, vm, w_pool_bf16, pool_scale, w_out_bf16, seq, tm=OUT_TM):
    t, d = x.shape
    tiles_per_seq = seq // tm
    halo_blocks = tm // POOL_HALO
    mem = pl.BlockSpec((None, N_MEM, GROUP_WIDTH), lambda i: (i // tiles_per_seq, 0, 0))
    est = (2 * tm * d * 2 + 4 * N_MEM * GROUP_WIDTH * 2 + 4 * tm * d * 4 + w_out_bf16.size * 2
           + w_pool_bf16.size * 2 + 8 * tm * POOL_WIDTH * 4)
    return pl.pallas_call(
        functools.partial(_pool_out_kernel, tiles_per_seq=tiles_per_seq),
        grid=(t // tm,),
        in_specs=[
            pl.BlockSpec((tm, POOL_WIDTH), lambda i: (i, 0)),
            pl.BlockSpec((POOL_HALO, POOL_WIDTH), lambda i: (jnp.maximum(i * halo_blocks - 1, 0), 0)),
            pl.BlockSpec((tm, GROUP_WIDTH), lambda i: (i, 0)),
            mem, mem,
            pl.BlockSpec((tm, d), lambda i: (i, 0)),
            _resident(w_pool_bf16.shape),
            _resident((1, POOL_WIDTH)),
            _resident(w_out_bf16.shape),
        ],
        out_specs=pl.BlockSpec((tm, d), lambda i: (i, 0)),
        out_shape=jax.ShapeDtypeStruct((t, d), F32),
        compiler_params=pltpu.CompilerParams(
            dimension_semantics=("arbitrary",), vmem_limit_bytes=_vmem_limit(est)),
        name="pool_out",
    )(u, u, qm, km, vm, x, w_pool_bf16, pool_scale.reshape(1, POOL_WIDTH), w_out_bf16)


def _swiglu_step(hn, wg_ref, wu_ref, wd_ref):
    gate = jnp.dot(hn, wg_ref[...], preferred_element_type=F32)
    up = jnp.dot(hn, wu_ref[...], preferred_element_type=F32)
    act = (gate * (1.0 / (1.0 + jnp.exp(-gate))) * up).astype(BF16)
    return jnp.dot(act, wd_ref[...], preferred_element_type=F32)


def _block_major(w_gu, tf):
    *lead, d, n = w_gu.shape
    w = w_gu.astype(BF16).reshape(*lead, d, n // tf, tf)
    return jnp.swapaxes(w, -3, -2)


def _ffn_dense_kernel(x_ref, g_ref, wg_ref, wu_ref, wd_ref, y_ref, hn_ref):
    @pl.when(pl.program_id(1) == 0)
    def _():
        x = x_ref[...]
        hn_ref[...] = (_rms(x) * g_ref[...]).astype(BF16)
        y_ref[...] = x

    y_ref[...] += _swiglu_step(hn_ref[...], wg_ref, wu_ref, wd_ref)


def _ffn_dense(x, gain, w_gu_blocks, w_down_bf16, tm=FFN_TM):
    t, d = x.shape
    _, _, tf = w_gu_blocks.shape
    nf = D_FF // tf
    est = 4 * tm * d * 4 + tm * d * 2 + 2 * 3 * d * tf * 2 + 3 * tm * tf * 4
    return pl.pallas_call(
        _ffn_dense_kernel,
        grid=(t // tm, nf),
        in_specs=[
            pl.BlockSpec((tm, d), lambda i, f: (i, 0)),
            _resident((1, d)),
            pl.BlockSpec((None, d, tf), lambda i, f: (f, 0, 0)),
            pl.BlockSpec((None, d, tf), lambda i, f: (f + nf, 0, 0)),
            pl.BlockSpec((tf, d), lambda i, f: (f, 0)),
        ],
        out_specs=pl.BlockSpec((tm, d), lambda i, f: (i, 0)),
        out_shape=jax.ShapeDtypeStruct((t, d), F32),
        scratch_shapes=[pltpu.VMEM((tm, d), BF16)],
        compiler_params=pltpu.CompilerParams(
            dimension_semantics=("arbitrary", "arbitrary"), vmem_limit_bytes=_vmem_limit(est)),
        name="ffn_dense",
    )(x, gain.reshape(1, d), w_gu_blocks, w_gu_blocks, w_down_bf16)


def _route_kernel(x_ref, g_ref, whi_ref, wlo_ref, tri_ref, hp_ref, info_ref, cnt_ref, carry_ref):
    i = pl.program_id(0)

    @pl.when(i == 0)
    def _():
        carry_ref[...] = jnp.zeros_like(carry_ref)

    hn = _rms(x_ref[...]) * g_ref[...]
    hi = hn.astype(BF16)
    hi32 = hi.astype(F32)
    lo = (hn - hi32).astype(BF16)
    logits = (jnp.dot(hi, whi_ref[...], preferred_element_type=F32)
              + jnp.dot(hi, wlo_ref[...], preferred_element_type=F32)
              + jnp.dot(lo, whi_ref[...], preferred_element_type=F32))
    tm = logits.shape[0]
    lane = lax.broadcasted_iota(I32, (tm, LANES), 1).astype(F32)
    lg = jnp.where(lane < N_EXPERTS, logits, -jnp.inf)
    m1 = jnp.max(lg, axis=-1, keepdims=True)
    i1 = jnp.min(jnp.where(lg == m1, lane, float(LANES)), axis=-1, keepdims=True)
    lg2 = jnp.where(lane == i1, -jnp.inf, lg)
    m2 = jnp.max(lg2, axis=-1, keepdims=True)
    i2 = jnp.min(jnp.where(lg2 == m2, lane, float(LANES)), axis=-1, keepdims=True)
    e = jnp.exp(m2 - m1)
    g1 = 1.0 / (1.0 + e)
    g2 = e / (1.0 + e)
    sel1 = lane == i1
    sel2 = lane == i2
    onehot = jnp.where(jnp.logical_or(sel1, sel2), 1.0, 0.0)
    before = jnp.dot(tri_ref[...], onehot.astype(BF16), preferred_element_type=F32) + carry_ref[0:1, :]
    r1 = jnp.sum(jnp.where(sel1, before, 0.0), axis=-1, keepdims=True)
    r2 = jnp.sum(jnp.where(sel2, before, 0.0), axis=-1, keepdims=True)
    carry_ref[0:1, :] = carry_ref[0:1, :] + jnp.sum(onehot, axis=0, keepdims=True)
    cnt_ref[...] = carry_ref[...]

    info = jnp.where(lane == 0, i1, 0.0)
    info = jnp.where(lane == 1, i2, info)
    info = jnp.where(lane == 2, r1, info)
    info = jnp.where(lane == 3, r2, info)
    info = jnp.where(lane == 4, g1, info)
    info = jnp.where(lane == 5, g2, info)
    info_ref[...] = info

    bits = pltpu.bitcast(hi32, U32)
    half = bits.shape[1] // 2
    words = (bits[:, :half] & jnp.uint32(0xFFFF0000)) | (bits[:, half:] >> 16)
    for s in range(SUBLANES):
        hp_ref[:, s, :] = words[:, s * LANES:(s + 1) * LANES]


def _route(x, gain, w_router, tm=ROUTE_TM):
    t, d = x.shape
    wpad = jnp.zeros((d, LANES), F32).at[:, :N_EXPERTS].set(w_router)
    whi = wpad.astype(BF16)
    wlo = (wpad - whi.astype(F32)).astype(BF16)
    tri = (lax.broadcasted_iota(I32, (tm, tm), 0) > lax.broadcasted_iota(I32, (tm, tm), 1)).astype(BF16)
    est = 4 * tm * d * 4 + 2 * tm * d * 2 + 2 * d * LANES * 2 + tm * tm * 2 + 8 * tm * LANES * 4
    return pl.pallas_call(
        _route_kernel,
        grid=(t // tm,),
        in_specs=[
            pl.BlockSpec((tm, d), lambda i: (i, 0)),
            _resident((1, d)),
            _resident((d, LANES)),
            _resident((d, LANES)),
            _resident((tm, tm)),
        ],
        out_specs=[
            pl.BlockSpec((tm, SUBLANES, LANES), lambda i: (i, 0, 0)),
            pl.BlockSpec((tm, LANES), lambda i: (i, 0)),
            pl.BlockSpec((SUBLANES, LANES), lambda i: (0, 0)),
        ],
        out_shape=[
            jax.ShapeDtypeStruct((t, SUBLANES, LANES), U32),
            jax.ShapeDtypeStruct((t, LANES), F32),
            jax.ShapeDtypeStruct((SUBLANES, LANES), F32),
        ],
        scratch_shapes=[pltpu.VMEM((SUBLANES, LANES), F32)],
        compiler_params=pltpu.CompilerParams(
            dimension_semantics=("arbitrary",), vmem_limit_bytes=_vmem_limit(est)),
        name="route",
    )(x, gain.reshape(1, d), whi, wlo, tri)


def _for_each_row(tm, fn):
    def body(c, carry):
        for u in range(ROW_UNROLL):
            fn(c * ROW_UNROLL + u)
        return carry

    lax.fori_loop(0, tm // ROW_UNROLL, body, 0)


def _scatter_kernel(pos_ref, hp_ref, xs_hbm_in, xs_hbm, sem):
    del xs_hbm_in
    tm = hp_ref.shape[0]

    def row_copy(r, k):
        return pltpu.make_async_copy(hp_ref.at[r], xs_hbm.at[pos_ref[0, 2 * r + k]], sem)

    def start(r):
        row_copy(r, 0).start()
        row_copy(r, 1).start()

    def wait(r):
        row_copy(r, 0).wait()
        row_copy(r, 1).wait()

    _for_each_row(tm, start)
    _for_each_row(tm, wait)


def _scatter_rows(hp, pos, n_slots, tm=ROW_TM):
    t = hp.shape[0]
    pos3 = pos.reshape(t // tm, 1, 2 * tm)
    return pl.pallas_call(
        _scatter_kernel,
        grid=(t // tm,),
        in_specs=[
            pl.BlockSpec((None, 1, 2 * tm), lambda i: (i, 0, 0), memory_space=pltpu.SMEM),
            pl.BlockSpec((tm,) + hp.shape[1:], lambda i: (i, 0, 0)),
            pl.BlockSpec(memory_space=pl.ANY),
        ],
        out_specs=pl.BlockSpec(memory_space=pl.ANY),
        out_shape=jax.ShapeDtypeStruct((n_slots,) + hp.shape[1:], U32),
        scratch_shapes=[pltpu.SemaphoreType.DMA(())],
        input_output_aliases={2: 0},
        compiler_params=pltpu.CompilerParams(dimension_semantics=("arbitrary",)),
        name="scatter_rows",
    )(pos3, hp, jnp.zeros((n_slots,) + hp.shape[1:], U32))


def _combine_kernel(pos_ref, x_ref, info_ref, ys_hbm, y_ref, buf_ref, sem):
    tm = x_ref.shape[0]

    def row_copy(r, k):
        return pltpu.make_async_copy(ys_hbm.at[pos_ref[0, 2 * r + k]], buf_ref.at[k, r], sem)

    def start(r):
        row_copy(r, 0).start()
        row_copy(r, 1).start()

    def wait(r):
        row_copy(r, 0).wait()
        row_copy(r, 1).wait()

    _for_each_row(tm, start)
    _for_each_row(tm, wait)
    info = info_ref[...]
    g1, g2 = info[:, 4:5], info[:, 5:6]
    for s in range(buf_ref.shape[2]):
        cols = slice(s * LANES, (s + 1) * LANES)
        y_ref[:, cols] = x_ref[:, cols] + g1 * buf_ref[0, :, s, :] + g2 * buf_ref[1, :, s, :]


def _combine(x, info, ys, pos, tm=ROW_TM):
    t, d = x.shape
    pos3 = pos.reshape(t // tm, 1, 2 * tm)
    est = 4 * tm * d * 4 + 2 * tm * LANES * 4 + 2 * tm * d * 4 + 2 * tm * d * 4
    return pl.pallas_call(
        _combine_kernel,
        grid=(t // tm,),
        in_specs=[
            pl.BlockSpec((None, 1, 2 * tm), lambda i: (i, 0, 0), memory_space=pltpu.SMEM),
            pl.BlockSpec((tm, d), lambda i: (i, 0)),
            pl.BlockSpec((tm, LANES), lambda i: (i, 0)),
            pl.BlockSpec(memory_space=pl.ANY),
        ],
        out_specs=pl.BlockSpec((tm, d), lambda i: (i, 0)),
        out_shape=jax.ShapeDtypeStruct((t, d), F32),
        scratch_shapes=[pltpu.VMEM((2, tm) + ys.shape[1:], F32), pltpu.SemaphoreType.DMA(())],
        compiler_params=pltpu.CompilerParams(
            dimension_semantics=("arbitrary",), vmem_limit_bytes=_vmem_limit(est)),
        name="combine",
    )(pos3, x, info, ys)


def _ffn_moe_kernel(te_ref, tr_ref, xs_ref, wg_ref, wu_ref, wd_ref, y_ref, hn_ref, acc_ref):
    del te_ref
    i, f = pl.program_id(0), pl.program_id(1)
    rows = tr_ref[i]
    tm, d = acc_ref.shape
    half_rows = tm // 2

    @pl.when(f == 0)
    def _():
        acc_ref[...] = jnp.zeros_like(acc_ref)

    @pl.when(jnp.logical_and(rows > 0, f == 0))
    def _():
        for s in range(SUBLANES):
            words = xs_ref[:, s, :]
            hn_ref[:, s * LANES:(s + 1) * LANES] = pltpu.bitcast(words & jnp.uint32(0xFFFF0000), F32).astype(BF16)
            hn_ref[:, d // 2 + s * LANES:d // 2 + (s + 1) * LANES] = pltpu.bitcast(words << 16, F32).astype(BF16)

    @pl.when(rows > half_rows)
    def _():
        acc_ref[...] += _swiglu_step(hn_ref[...], wg_ref, wu_ref, wd_ref)

    @pl.when(jnp.logical_and(rows > 0, rows <= half_rows))
    def _():
        acc_ref[0:half_rows, :] += _swiglu_step(hn_ref[0:half_rows, :], wg_ref, wu_ref, wd_ref)

    @pl.when(f == pl.num_programs(1) - 1)
    def _():
        for s in range(d // LANES):
            y_ref[:, s, :] = acc_ref[:, s * LANES:(s + 1) * LANES]


def _ffn_moe(xs, tile_expert, tile_rows, w_gu_blocks, w_down_bf16, tm=FFN_TM):
    n_slots = xs.shape[0]
    _, _, d, tf = w_gu_blocks.shape
    nf = D_FF // tf
    n_tiles = n_slots // tm

    def f_eff(i, f, tr):
        return jnp.where(tr[i] > 0, f, nf - 1)

    est = 2 * tm * d * 2 + 2 * tm * d * 4 + tm * d * 2 + tm * d * 4 + 2 * 3 * d * tf * 2 + 3 * tm * tf * 4
    return pl.pallas_call(
        _ffn_moe_kernel,
        grid_spec=pltpu.PrefetchScalarGridSpec(
            num_scalar_prefetch=2,
            grid=(n_tiles, nf),
            in_specs=[
                pl.BlockSpec((tm,) + xs.shape[1:], lambda i, f, te, tr: (i, 0, 0)),
                pl.BlockSpec((None, None, d, tf), lambda i, f, te, tr: (te[i], f_eff(i, f, tr), 0, 0)),
                pl.BlockSpec((None, None, d, tf), lambda i, f, te, tr: (te[i], f_eff(i, f, tr) + nf, 0, 0)),
                pl.BlockSpec((None, tf, d), lambda i, f, te, tr: (te[i], f_eff(i, f, tr), 0)),
            ],
            out_specs=pl.BlockSpec((tm, d // LANES, LANES), lambda i, f, te, tr: (i, 0, 0)),
            scratch_shapes=[pltpu.VMEM((tm, d), BF16), pltpu.VMEM((tm, d), F32)],
        ),
        out_shape=jax.ShapeDtypeStruct((n_slots, d // LANES, LANES), F32),
        compiler_params=pltpu.CompilerParams(
            dimension_semantics=("arbitrary", "arbitrary"), vmem_limit_bytes=_vmem_limit(est)),
        name="ffn_moe",
    )(tile_expert, tile_rows, xs, w_gu_blocks, w_gu_blocks, w_down_bf16)


def _moe_layer(x, gain, w_router, w_gu_blocks, w_down_bf16, tm=FFN_TM):
    t, _ = x.shape
    hp, info, cnt = _route(x, gain, w_router)
    counts = cnt[0, :N_EXPERTS].astype(I32)
    tiles_per_expert = (counts + tm - 1) // tm
    tile_end = jnp.cumsum(tiles_per_expert)
    group_start = (tile_end - tiles_per_expert) * tm
    n_tiles = (2 * t + N_EXPERTS * (tm - 1)) // tm
    tile_ids = jnp.arange(n_tiles, dtype=I32)
    tile_expert = jnp.minimum(jnp.sum(tile_ids[:, None] >= tile_end[None, :], axis=1), N_EXPERTS - 1).astype(I32)
    last_used_expert = tile_expert[jnp.maximum(tile_end[-1] - 1, 0)]
    used = tile_ids < tile_end[-1]
    tile_expert = jnp.where(used, tile_expert, last_used_expert)
    rows_left = counts[tile_expert] - (tile_ids * tm - group_start[tile_expert])
    tile_rows = jnp.where(used, jnp.clip(rows_left, 0, tm), 0).astype(I32)
    idx = info[:, 0:2].astype(I32)
    rank = info[:, 2:4].astype(I32)
    pos = (group_start[idx] + rank).astype(I32)

    xs = _scatter_rows(hp, pos, n_tiles * tm)
    ys = _ffn_moe(xs, tile_expert, tile_rows, w_gu_blocks, w_down_bf16)
    return _combine(x, info, ys, pos)


def _mem_kv(mem2d, norm_mem, w_mem_kv, k_gain):
    head_gain = jnp.concatenate([jnp.tile(k_gain, HEADS_PER_GROUP), jnp.ones((GROUP_WIDTH,), F32)])[None, :]
    segments = ((0, GROUP_WIDTH, True, 1), (GROUP_WIDTH, GROUP_WIDTH, False, 1))
    km, vm = _proj(mem2d, norm_mem, w_mem_kv.astype(BF16), head_gain, segments, seq=mem2d.shape[0])
    return km.reshape(-1, N_MEM, GROUP_WIDTH), vm.reshape(-1, N_MEM, GROUP_WIDTH)


def _dilated_mixer(x, mem2d, seq, norm_mix, norm_mem, w_in, qk_norm, w_mem_kv, w_out):
    n_grp = len(DILATIONS)
    head_gain = jnp.concatenate([
        jnp.tile(qk_norm[0], n_grp * HEADS_PER_GROUP), jnp.tile(qk_norm[1], n_grp * HEADS_PER_GROUP),
        jnp.ones((n_grp * GROUP_WIDTH,), F32), jnp.tile(qk_norm[2], HEADS_PER_GROUP)])[None, :]
    segments = tuple((c * GROUP_WIDTH, GROUP_WIDTH, c < 2 * n_grp, DILATIONS[c % n_grp]) for c in range(3 * n_grp))
    segments += ((3 * n_grp * GROUP_WIDTH, GROUP_WIDTH, True, 1),)
    *qkv, qm = _proj(x, norm_mix, w_in.astype(BF16), head_gain, segments, seq)
    km, vm = _mem_kv(mem2d, norm_mem, w_mem_kv, qk_norm[3])
    batch = x.shape[0] // seq
    attn = []
    for g, dilation in enumerate(DILATIONS):
        q, k, v = (a.reshape(batch, dilation, seq // dilation, GROUP_WIDTH) for a in qkv[g::n_grp])
        attn.append(_dil_attn(q, k, v))
    return _dil_out(x, attn, qm, km, vm, w_out.astype(BF16), seq)


def _pooling_mixer(x, mem2d, seq, norm_mix, norm_mem, w_in, w_pool, pool_scale, qk_norm, w_mem_kv, w_out):
    head_gain = jnp.concatenate([jnp.ones((POOL_WIDTH,), F32), jnp.tile(qk_norm[0], HEADS_PER_GROUP)])[None, :]
    segments = ((0, POOL_WIDTH, False, 1), (POOL_WIDTH, GROUP_WIDTH, True, 1))
    u, qm = _proj(x, norm_mix, w_in.astype(BF16), head_gain, segments, seq)
    km, vm = _mem_kv(mem2d, norm_mem, w_mem_kv, qk_norm[1])
    return _pool_out(x, u, qm, km, vm, w_pool.astype(BF16), pool_scale, w_out.astype(BF16), seq)


def kernel(x, mem, l0_norm_mix, l0_norm_mem, l0_w_in, l0_qk_norm, l0_w_mem_kv, l0_w_out, l0_norm_ffn, l0_w_gu, l0_w_down, l1_norm_mix, l1_norm_mem, l1_w_in, l1_w_pool, l1_pool_scale, l1_qk_norm, l1_w_mem_kv, l1_w_out, l1_norm_ffn, l1_w_router, l1_w_gu_e, l1_w_down_e, l2_norm_mix, l2_norm_mem, l2_w_in, l2_qk_norm, l2_w_mem_kv, l2_w_out, l2_norm_ffn, l2_w_gu, l2_w_down, l3_norm_mix, l3_norm_mem, l3_w_in, l3_w_pool, l3_pool_scale, l3_qk_norm, l3_w_mem_kv, l3_w_out, l3_norm_ffn, l3_w_router, l3_w_gu_e, l3_w_down_e):
    batch, seq, d = x.shape
    h = x.reshape(batch * seq, d)
    mem2d = mem.reshape(-1, d)

    h = _dilated_mixer(h, mem2d, seq, l0_norm_mix, l0_norm_mem, l0_w_in, l0_qk_norm, l0_w_mem_kv, l0_w_out)
    h = _ffn_dense(h, l0_norm_ffn, _block_major(l0_w_gu, FFN_TF), l0_w_down.astype(BF16))
    h = _pooling_mixer(h, mem2d, seq, l1_norm_mix, l1_norm_mem, l1_w_in, l1_w_pool, l1_pool_scale,
                       l1_qk_norm, l1_w_mem_kv, l1_w_out)
    h = _moe_layer(h, l1_norm_ffn, l1_w_router, _block_major(l1_w_gu_e, FFN_TF), l1_w_down_e.astype(BF16))
    h = _dilated_mixer(h, mem2d, seq, l2_norm_mix, l2_norm_mem, l2_w_in, l2_qk_norm, l2_w_mem_kv, l2_w_out)
    h = _ffn_dense(h, l2_norm_ffn, _block_major(l2_w_gu, FFN_TF), l2_w_down.astype(BF16))
    h = _pooling_mixer(h, mem2d, seq, l3_norm_mix, l3_norm_mem, l3_w_in, l3_w_pool, l3_pool_scale,
                       l3_qk_norm, l3_w_mem_kv, l3_w_out)
    h = _moe_layer(h, l3_norm_ffn, l3_w_router, _block_major(l3_w_gu_e, FFN_TF), l3_w_down_e.astype(BF16))
    return h.reshape(batch, seq, d)
```

```python
import functools

import jax
import jax.numpy as jnp
from jax import lax
from jax.experimental import pallas as pl
from jax.experimental.pallas import tpu as pltpu

F32 = jnp.float32
BF16 = jnp.bfloat16
U32 = jnp.uint32
I32 = jnp.int32

HEAD_DIM = 128
HEADS_PER_GROUP = 4
GROUP_WIDTH = HEADS_PER_GROUP * HEAD_DIM
DILATIONS = (1, 4, 16)
ATTN_BLOCK = 128
N_MEM = 256
POOL_WINDOWS = (2, 4, 8, 16)
POOL_GROUP = 384
POOL_WIDTH = 4 * POOL_GROUP
POOL_HALO = 16
D_FF = 7168
N_EXPERTS = 8
EPS = 1e-6
SCORE_SCALE = HEAD_DIM ** -0.5

LANES = 128
SUBLANES = 8
V7X_VMEM_BYTES = 64 * 1024 * 1024

PROJ_TM = 512
ATTN_QB = 512
OUT_TM = 512
FFN_TM, FFN_TF = 1024, 512
MOE_TF = 256
ROUTE_TM = 512
ROW_TM = 256
ROW_UNROLL = 8


def _vmem_limit(nbytes):
    return int(min(nbytes * 5 // 4 + (6 << 20), V7X_VMEM_BYTES - (4 << 20)))


def _rms(x, eps=EPS):
    return x * lax.rsqrt(jnp.mean(x * x, axis=-1, keepdims=True) + eps)


def _resident(shape):
    return pl.BlockSpec(shape, lambda *_: (0,) * len(shape), pipeline_mode=pl.Buffered(1))


def _proj_kernel(x_ref, g_ref, w_ref, hg_ref, *refs, segments):
    outs, (hn_ref, stage_ref) = refs[:len(segments)], refs[len(segments):]
    tm = x_ref.shape[0]
    hn_ref[...] = (_rms(x_ref[...]) * g_ref[...]).astype(BF16)
    for (col0, width, norm, dilation), o_ref in zip(segments, outs):
        for sub in range(width // GROUP_WIDTH):
            c0 = col0 + sub * GROUP_WIDTH
            acc = jnp.dot(hn_ref[...], w_ref[:, c0:c0 + GROUP_WIDTH], preferred_element_type=F32)
            for h in range(HEADS_PER_GROUP):
                cols = slice(h * HEAD_DIM, (h + 1) * HEAD_DIM)
                val = acc[:, cols]
                if norm:
                    val = _rms(val) * hg_ref[:, c0 + h * HEAD_DIM:c0 + (h + 1) * HEAD_DIM]
                if dilation == 1:
                    o_ref[:, sub * GROUP_WIDTH + h * HEAD_DIM:sub * GROUP_WIDTH + (h + 1) * HEAD_DIM] = val.astype(BF16)
                else:
                    stage_ref[h] = val
                    for r in range(dilation):
                        o_ref[r, :, cols] = stage_ref[h, pl.ds(r, tm // dilation, stride=dilation), :].astype(BF16)


def _proj(x, gain, w_bf16, head_gain, segments, seq, tm=PROJ_TM):
    t, d = x.shape
    tm = min(tm, t)
    tiles_per_seq = max(seq // tm, 1)
    out_specs, out_shapes = [], []
    for _, width, _, dilation in segments:
        if dilation == 1:
            out_specs.append(pl.BlockSpec((tm, width), lambda i: (i, 0)))
            out_shapes.append(jax.ShapeDtypeStruct((t, width), BF16))
        else:
            out_specs.append(pl.BlockSpec((None, dilation, tm // dilation, width),
                                          lambda i: (i // tiles_per_seq, 0, i % tiles_per_seq, 0)))
            out_shapes.append(jax.ShapeDtypeStruct((t // seq, dilation, seq // dilation, width), BF16))
    n_out_cols = sum(s[1] for s in segments)
    est = (2 * tm * d * 4 + tm * d * 2 + w_bf16.size * 2 + 4 * tm * n_out_cols + tm * GROUP_WIDTH * 4
           + 4 * tm * GROUP_WIDTH * 4)
    return pl.pallas_call(
        functools.partial(_proj_kernel, segments=tuple(segments)),
        grid=(t // tm,),
        in_specs=[
            pl.BlockSpec((tm, d), lambda i: (i, 0)),
            _resident((1, d)),
            _resident(w_bf16.shape),
            _resident(head_gain.shape),
        ],
        out_specs=out_specs,
        out_shape=out_shapes,
        scratch_shapes=[pltpu.VMEM((tm, d), BF16), pltpu.VMEM((HEADS_PER_GROUP, tm, HEAD_DIM), F32)],
        compiler_params=pltpu.CompilerParams(
            dimension_semantics=("arbitrary",), vmem_limit_bytes=_vmem_limit(est)),
        name="proj",
    )(x, gain.reshape(1, d), w_bf16, head_gain)


def _dil_attn_kernel(q_ref, kp_ref, k_ref, vp_ref, v_ref, o_ref, l_ref, *, qb):
    has_prev = pl.program_id(2) > 0
    qi = lax.broadcasted_iota(I32, (ATTN_BLOCK, 2 * ATTN_BLOCK), 0)
    kj = lax.broadcasted_iota(I32, (ATTN_BLOCK, 2 * ATTN_BLOCK), 1)
    in_prev = kj < ATTN_BLOCK
    kk = jnp.where(in_prev, kj, kj - ATTN_BLOCK)
    band = jnp.logical_and(kk >= jnp.where(in_prev, qi, 0), kk <= jnp.where(in_prev, ATTN_BLOCK, qi))
    band_first = jnp.logical_and(band, jnp.logical_or(has_prev, jnp.logical_not(in_prev)))
    for jb in range(qb // ATTN_BLOCK):
        rows = slice(jb * ATTN_BLOCK, (jb + 1) * ATTN_BLOCK)
        for h in range(HEADS_PER_GROUP):
            cols = slice(h * HEAD_DIM, (h + 1) * HEAD_DIM)
            q = q_ref[rows, cols]
            if jb == 0:
                keys = jnp.concatenate([kp_ref[:, cols], k_ref[rows, cols]], axis=0)
                vals = jnp.concatenate([vp_ref[:, cols], v_ref[rows, cols]], axis=0)
                mask = band_first
            else:
                both = slice((jb - 1) * ATTN_BLOCK, (jb + 1) * ATTN_BLOCK)
                keys = k_ref[both, cols]
                vals = v_ref[both, cols]
                mask = band
            s = lax.dot_general(q, keys, (((1,), (1,)), ((), ())), preferred_element_type=F32)
            s = jnp.where(mask, s * SCORE_SCALE, -jnp.inf)
            m = jnp.max(s, axis=-1, keepdims=True)
            p = jnp.exp(s - m)
            den = jnp.sum(p, axis=-1, keepdims=True)
            o = jnp.dot(p.astype(BF16), vals, preferred_element_type=F32)
            o_ref[rows, cols] = o / den
            l_ref[rows, cols] = jnp.broadcast_to(m + jnp.log(den), (ATTN_BLOCK, HEAD_DIM))


def _dil_attn(q, k, v, qb=ATTN_QB):
    batch, dilation, length, width = q.shape
    qb = min(qb, length)
    bpq = qb // ATTN_BLOCK
    cur = pl.BlockSpec((None, None, qb, width), lambda b, r, i: (b, r, i, 0))
    prev = pl.BlockSpec((None, None, ATTN_BLOCK, width), lambda b, r, i: (b, r, jnp.maximum(i * bpq - 1, 0), 0))
    out_sds = jax.ShapeDtypeStruct(q.shape, F32)
    est = 2 * (3 * qb + 2 * ATTN_BLOCK) * width * 2 + 2 * 2 * qb * width * 4
    return pl.pallas_call(
        functools.partial(_dil_attn_kernel, qb=qb),
        grid=(batch, dilation, length // qb),
        in_specs=[cur, prev, cur, prev, cur],
        out_specs=[cur, cur],
        out_shape=[out_sds, out_sds],
        compiler_params=pltpu.CompilerParams(
            dimension_semantics=("arbitrary", "arbitrary", "arbitrary"),
            vmem_limit_bytes=_vmem_limit(est)),
        name="dil_attn",
    )(q, k, k, v, v)


def _mem_attention(qm_ref, km_ref, vm_ref):
    outs = []
    for h in range(HEADS_PER_GROUP):
        cols = slice(h * HEAD_DIM, (h + 1) * HEAD_DIM)
        s = lax.dot_general(qm_ref[:, cols], km_ref[:, cols], (((1,), (1,)), ((), ())),
                            preferred_element_type=F32) * SCORE_SCALE
        m = jnp.max(s, axis=-1, keepdims=True)
        p = jnp.exp(s - m)
        den = jnp.sum(p, axis=-1, keepdims=True)
        o = jnp.dot(p.astype(BF16), vm_ref[:, cols], preferred_element_type=F32)
        outs.append((o / den).astype(BF16))
    return outs


def _to_token_order(src_ref, dst_ref):
    dilation, per_res, _ = src_ref.shape
    for h in range(HEADS_PER_GROUP):
        for r in range(dilation):
            dst_ref[h, pl.ds(r, per_res, stride=dilation), :] = src_ref[r, :, h * HEAD_DIM:(h + 1) * HEAD_DIM]


def _dil_out_kernel(o0_ref, l0_ref, o1_ref, l1_ref, o2_ref, l2_ref, qm_ref, km_ref, vm_ref,
                    x_ref, w_ref, y_ref, o1s, l1s, o2s, l2s, dil_ref):
    for src, dst in ((o1_ref, o1s), (l1_ref, l1s), (o2_ref, o2s), (l2_ref, l2s)):
        _to_token_order(src, dst)
    for h in range(HEADS_PER_GROUP):
        cols = slice(h * HEAD_DIM, (h + 1) * HEAD_DIM)
        l0, l1, l2 = l0_ref[:, cols], l1s[h], l2s[h]
        mx = jnp.maximum(jnp.maximum(l0, l1), l2)
        e0, e1, e2 = jnp.exp(l0 - mx), jnp.exp(l1 - mx), jnp.exp(l2 - mx)
        den = e0 + e1 + e2
        dil = (e0 / den) * o0_ref[:, cols] + (e1 / den) * o1s[h] + (e2 / den) * o2s[h]
        dil_ref[:, cols] = dil.astype(BF16)
    y = jnp.dot(dil_ref[...], w_ref[0:GROUP_WIDTH, :], preferred_element_type=F32)
    for h, mo in enumerate(_mem_attention(qm_ref, km_ref, vm_ref)):
        rows = slice(GROUP_WIDTH + h * HEAD_DIM, GROUP_WIDTH + (h + 1) * HEAD_DIM)
        y = y + jnp.dot(mo, w_ref[rows, :], preferred_element_type=F32)
    y_ref[...] = x_ref[...] + y


def _dil_out(x, attn, qm, km, vm, w_out_bf16, seq, tm=OUT_TM):
    t, d = x.shape
    tiles_per_seq = seq // tm

    def grp(dilation):
        return pl.BlockSpec((None, dilation, tm // dilation, GROUP_WIDTH),
                            lambda i: (i // tiles_per_seq, 0, i % tiles_per_seq, 0))

    tok = pl.BlockSpec((tm, GROUP_WIDTH), lambda i: (i, 0))
    mem = pl.BlockSpec((None, N_MEM, GROUP_WIDTH), lambda i: (i // tiles_per_seq, 0, 0))
    (o0, l0), (o1, l1), (o2, l2) = attn
    stage = pltpu.VMEM((HEADS_PER_GROUP, tm, HEAD_DIM), F32)
    est = (2 * 6 * tm * GROUP_WIDTH * 4 + 4 * tm * GROUP_WIDTH * 4 + 2 * tm * GROUP_WIDTH * 2
           + 4 * N_MEM * GROUP_WIDTH * 2 + 4 * tm * d * 4 + w_out_bf16.size * 2 + 3 * tm * d * 4)
    return pl.pallas_call(
        _dil_out_kernel,
        grid=(t // tm,),
        in_specs=[tok, tok, grp(DILATIONS[1]), grp(DILATIONS[1]), grp(DILATIONS[2]), grp(DILATIONS[2]),
                  tok, mem, mem, pl.BlockSpec((tm, d), lambda i: (i, 0)), _resident(w_out_bf16.shape)],
        out_specs=pl.BlockSpec((tm, d), lambda i: (i, 0)),
        out_shape=jax.ShapeDtypeStruct((t, d), F32),
        scratch_shapes=[stage, stage, stage, stage, pltpu.VMEM((tm, GROUP_WIDTH), BF16)],
        compiler_params=pltpu.CompilerParams(
            dimension_semantics=("arbitrary",), vmem_limit_bytes=_vmem_limit(est)),
        name="dil_out",
    )(o0.reshape(t, GROUP_WIDTH), l0.reshape(t, GROUP_WIDTH), o1, l1, o2, l2, qm, km, vm, x, w_out_bf16)


def _pool_out_kernel(u_ref, halo_ref, qm_ref, km_ref, vm_ref, x_ref, wp_ref, ps_ref, w_ref, y_ref,
                     *, tiles_per_seq):
    tm = u_ref.shape[0]
    tile_in_seq = pl.program_id(0) % tiles_per_seq
    halo = jnp.where(tile_in_seq > 0, halo_ref[...].astype(F32), 0.0)
    full = jnp.concatenate([halo, u_ref[...].astype(F32)], axis=0)
    pos = tile_in_seq * tm + lax.broadcasted_iota(I32, (tm, 1), 0)
    y = None
    acc = full
    for g, window in enumerate(POOL_WINDOWS):
        lo = g * POOL_GROUP
        acc = acc[:, (POOL_GROUP if g else 0):]
        acc = acc + pltpu.roll(acc, window // 2, 0)
        count = jnp.minimum(pos + 1, window).astype(F32)
        pooled = acc[POOL_HALO:, :POOL_GROUP] / count - full[POOL_HALO:, lo:lo + POOL_GROUP]
        po = jnp.dot(pooled.astype(BF16), wp_ref[g], preferred_element_type=F32)
        po = (po * ps_ref[:, lo:lo + POOL_GROUP]).astype(BF16)
        part = jnp.dot(po, w_ref[lo:lo + POOL_GROUP, :], preferred_element_type=F32)
        y = part if y is None else y + part
    for h, mo in enumerate(_mem_attention(qm_ref, km_ref, vm_ref)):
        rows = slice(POOL_WIDTH + h * HEAD_DIM, POOL_WIDTH + (h + 1) * HEAD_DIM)
        y = y + jnp.dot(mo, w_ref[rows, :], preferred_element_type=F32)
    y_ref[...] = x_ref[...] + y


def _pool_out(x, u, qm, km, vm, w_pool_bf16, pool_scale, w_out_bf16, seq, tm=OUT_TM):
    t, d = x.shape
    tiles_per_seq = seq // tm
    halo_blocks = tm // POOL_HALO
    mem = pl.BlockSpec((None, N_MEM, GROUP_WIDTH), lambda i: (i // tiles_per_seq, 0, 0))
    est = (2 * tm * d * 2 + 4 * N_MEM * GROUP_WIDTH * 2 + 4 * tm * d * 4 + w_out_bf16.size * 2
           + w_pool_bf16.size * 2 + 8 * tm * POOL_WIDTH * 4)
    return pl.pallas_call(
        functools.partial(_pool_out_kernel, tiles_per_seq=tiles_per_seq),
        grid=(t // tm,),
        in_specs=[
            pl.BlockSpec((tm, POOL_WIDTH), lambda i: (i, 0)),
            pl.BlockSpec((POOL_HALO, POOL_WIDTH), lambda i: (jnp.maximum(i * halo_blocks - 1, 0), 0)),
            pl.BlockSpec((tm, GROUP_WIDTH), lambda i: (i, 0)),
            mem, mem,
            pl.BlockSpec((tm, d), lambda i: (i, 0)),
            _resident(w_pool_bf16.shape),
            _resident((1, POOL_WIDTH)),
            _resident(w_out_bf16.shape),
        ],
        out_specs=pl.BlockSpec((tm, d), lambda i: (i, 0)),
        out_shape=jax.ShapeDtypeStruct((t, d), F32),
        compiler_params=pltpu.CompilerParams(
            dimension_semantics=("arbitrary",), vmem_limit_bytes=_vmem_limit(est)),
        name="pool_out",
    )(u, u, qm, km, vm, x, w_pool_bf16, pool_scale.reshape(1, POOL_WIDTH), w_out_bf16)


def _swiglu_step(hn, wg_ref, wu_ref, wd_ref):
    gate = jnp.dot(hn, wg_ref[...].astype(BF16), preferred_element_type=F32)
    up = jnp.dot(hn, wu_ref[...].astype(BF16), preferred_element_type=F32)
    act = (gate * (1.0 / (1.0 + jnp.exp(-gate))) * up).astype(BF16)
    return jnp.dot(act, wd_ref[...].astype(BF16), preferred_element_type=F32)


def _ffn_dense_kernel(x_ref, g_ref, wg_ref, wu_ref, wd_ref, y_ref, hn_ref):
    @pl.when(pl.program_id(1) == 0)
    def _():
        x = x_ref[...]
        hn_ref[...] = (_rms(x) * g_ref[...]).astype(BF16)
        y_ref[...] = x

    y_ref[...] += _swiglu_step(hn_ref[...], wg_ref, wu_ref, wd_ref)


def _ffn_dense(x, gain, w_gu, w_down, tm=FFN_TM, tf=FFN_TF):
    t, d = x.shape
    nf = D_FF // tf
    wbytes = w_gu.dtype.itemsize
    est = 4 * tm * d * 4 + tm * d * 2 + 2 * 3 * d * tf * wbytes + 3 * tm * tf * 4
    return pl.pallas_call(
        _ffn_dense_kernel,
        grid=(t // tm, nf),
        in_specs=[
            pl.BlockSpec((tm, d), lambda i, f: (i, 0)),
            _resident((1, d)),
            pl.BlockSpec((d, tf), lambda i, f: (0, f)),
            pl.BlockSpec((d, tf), lambda i, f: (0, f + nf)),
            pl.BlockSpec((tf, d), lambda i, f: (f, 0)),
        ],
        out_specs=pl.BlockSpec((tm, d), lambda i, f: (i, 0)),
        out_shape=jax.ShapeDtypeStruct((t, d), F32),
        scratch_shapes=[pltpu.VMEM((tm, d), BF16)],
        compiler_params=pltpu.CompilerParams(
            dimension_semantics=("arbitrary", "arbitrary"), vmem_limit_bytes=_vmem_limit(est)),
        name="ffn_dense",
    )(x, gain.reshape(1, d), w_gu, w_gu, w_down)


def _route_kernel(x_ref, g_ref, whi_ref, wlo_ref, tri_ref, hp_ref, info_ref, cnt_ref, carry_ref):
    i = pl.program_id(0)

    @pl.when(i == 0)
    def _():
        carry_ref[...] = jnp.zeros_like(carry_ref)

    hn = _rms(x_ref[...]) * g_ref[...]
    hi = hn.astype(BF16)
    hi32 = hi.astype(F32)
    lo = (hn - hi32).astype(BF16)
    logits = (jnp.dot(hi, whi_ref[...], preferred_element_type=F32)
              + jnp.dot(hi, wlo_ref[...], preferred_element_type=F32)
              + jnp.dot(lo, whi_ref[...], preferred_element_type=F32))
    tm = logits.shape[0]
    lane = lax.broadcasted_iota(I32, (tm, LANES), 1).astype(F32)
    lg = jnp.where(lane < N_EXPERTS, logits, -jnp.inf)
    m1 = jnp.max(lg, axis=-1, keepdims=True)
    i1 = jnp.min(jnp.where(lg == m1, lane, float(LANES)), axis=-1, keepdims=True)
    lg2 = jnp.where(lane == i1, -jnp.inf, lg)
    m2 = jnp.max(lg2, axis=-1, keepdims=True)
    i2 = jnp.min(jnp.where(lg2 == m2, lane, float(LANES)), axis=-1, keepdims=True)
    e = jnp.exp(m2 - m1)
    g1 = 1.0 / (1.0 + e)
    g2 = e / (1.0 + e)
    sel1 = lane == i1
    sel2 = lane == i2
    onehot = jnp.where(jnp.logical_or(sel1, sel2), 1.0, 0.0)
    before = jnp.dot(tri_ref[...], onehot.astype(BF16), preferred_element_type=F32) + carry_ref[0:1, :]
    r1 = jnp.sum(jnp.where(sel1, before, 0.0), axis=-1, keepdims=True)
    r2 = jnp.sum(jnp.where(sel2, before, 0.0), axis=-1, keepdims=True)
    carry_ref[0:1, :] = carry_ref[0:1, :] + jnp.sum(onehot, axis=0, keepdims=True)
    cnt_ref[...] = carry_ref[...]

    info = jnp.where(lane == 0, i1, 0.0)
    info = jnp.where(lane == 1, i2, info)
    info = jnp.where(lane == 2, r1, info)
    info = jnp.where(lane == 3, r2, info)
    info = jnp.where(lane == 4, g1, info)
    info = jnp.where(lane == 5, g2, info)
    info_ref[...] = info

    hp_ref[...] = _pack_halves(hi32)


def _route(x, gain, w_router, tm=ROUTE_TM):
    t, d = x.shape
    wpad = jnp.zeros((d, LANES), F32).at[:, :N_EXPERTS].set(w_router)
    whi = wpad.astype(BF16)
    wlo = (wpad - whi.astype(F32)).astype(BF16)
    tri = (lax.broadcasted_iota(I32, (tm, tm), 0) > lax.broadcasted_iota(I32, (tm, tm), 1)).astype(BF16)
    est = 4 * tm * d * 4 + 2 * tm * d * 2 + 2 * d * LANES * 2 + tm * tm * 2 + 8 * tm * LANES * 4
    return pl.pallas_call(
        _route_kernel,
        grid=(t // tm,),
        in_specs=[
            pl.BlockSpec((tm, d), lambda i: (i, 0)),
            _resident((1, d)),
            _resident((d, LANES)),
            _resident((d, LANES)),
            _resident((tm, tm)),
        ],
        out_specs=[
            pl.BlockSpec((tm, d // 2), lambda i: (i, 0)),
            pl.BlockSpec((tm, LANES), lambda i: (i, 0)),
            pl.BlockSpec((SUBLANES, LANES), lambda i: (0, 0)),
        ],
        out_shape=[
            jax.ShapeDtypeStruct((t, d // 2), U32),
            jax.ShapeDtypeStruct((t, LANES), F32),
            jax.ShapeDtypeStruct((SUBLANES, LANES), F32),
        ],
        scratch_shapes=[pltpu.VMEM((SUBLANES, LANES), F32)],
        compiler_params=pltpu.CompilerParams(
            dimension_semantics=("arbitrary",), vmem_limit_bytes=_vmem_limit(est)),
        name="route",
    )(x, gain.reshape(1, d), whi, wlo, tri)


def _pack_halves(x):
    bits = pltpu.bitcast(x.astype(BF16).astype(F32), U32)
    half = bits.shape[1] // 2
    return (bits[:, :half] & jnp.uint32(0xFFFF0000)) | (bits[:, half:] >> 16)


def _unpack_halves(words):
    return pltpu.bitcast(words & jnp.uint32(0xFFFF0000), F32), pltpu.bitcast(words << 16, F32)


def _for_each_row(tm, fn):
    def body(c, carry):
        for u in range(ROW_UNROLL):
            fn(c * ROW_UNROLL + u)
        return carry

    lax.fori_loop(0, tm // ROW_UNROLL, body, 0)


def _scatter_kernel(pos_ref, hp_ref, xs_hbm_in, xs_hbm, sem):
    del xs_hbm_in
    tm = hp_ref.shape[0]

    def row_copy(r, k):
        return pltpu.make_async_copy(hp_ref.at[pl.ds(r, 1)], xs_hbm.at[pl.ds(pos_ref[0, 2 * r + k], 1)], sem)

    def start(r):
        row_copy(r, 0).start()
        row_copy(r, 1).start()

    def wait(r):
        row_copy(r, 0).wait()
        row_copy(r, 1).wait()

    _for_each_row(tm, start)
    _for_each_row(tm, wait)


def _scatter_rows(hp, pos, n_slots, tm=ROW_TM):
    t = hp.shape[0]
    pos3 = pos.reshape(t // tm, 1, 2 * tm)
    return pl.pallas_call(
        _scatter_kernel,
        grid=(t // tm,),
        in_specs=[
            pl.BlockSpec((None, 1, 2 * tm), lambda i: (i, 0, 0), memory_space=pltpu.SMEM),
            pl.BlockSpec((tm, hp.shape[1]), lambda i: (i, 0)),
            pl.BlockSpec(memory_space=pl.ANY),
        ],
        out_specs=pl.BlockSpec(memory_space=pl.ANY),
        out_shape=jax.ShapeDtypeStruct((n_slots, hp.shape[1]), U32),
        scratch_shapes=[pltpu.SemaphoreType.DMA(())],
        input_output_aliases={2: 0},
        compiler_params=pltpu.CompilerParams(dimension_semantics=("arbitrary",)),
        name="scatter_rows",
    )(pos3, hp, jnp.zeros((n_slots, hp.shape[1]), U32))


def _combine_kernel(pos_ref, x_ref, info_ref, ys_hbm, y_ref, buf_ref, sem):
    tm, d = x_ref.shape

    def row_copy(r, k):
        return pltpu.make_async_copy(ys_hbm.at[pl.ds(pos_ref[0, 2 * r + k], 1)], buf_ref.at[k, pl.ds(r, 1)], sem)

    def start(r):
        row_copy(r, 0).start()
        row_copy(r, 1).start()

    def wait(r):
        row_copy(r, 0).wait()
        row_copy(r, 1).wait()

    _for_each_row(tm, start)
    _for_each_row(tm, wait)
    info = info_ref[...]
    g1, g2 = info[:, 4:5], info[:, 5:6]
    a_hi, a_lo = _unpack_halves(buf_ref[0])
    b_hi, b_lo = _unpack_halves(buf_ref[1])
    y_ref[:, :d // 2] = x_ref[:, :d // 2] + g1 * a_hi + g2 * b_hi
    y_ref[:, d // 2:] = x_ref[:, d // 2:] + g1 * a_lo + g2 * b_lo


def _combine(x, info, ys, pos, tm=ROW_TM):
    t, d = x.shape
    pos3 = pos.reshape(t // tm, 1, 2 * tm)
    est = 4 * tm * d * 4 + 2 * tm * LANES * 4 + 2 * tm * d * 2 + 2 * tm * d * 4
    return pl.pallas_call(
        _combine_kernel,
        grid=(t // tm,),
        in_specs=[
            pl.BlockSpec((None, 1, 2 * tm), lambda i: (i, 0, 0), memory_space=pltpu.SMEM),
            pl.BlockSpec((tm, d), lambda i: (i, 0)),
            pl.BlockSpec((tm, LANES), lambda i: (i, 0)),
            pl.BlockSpec(memory_space=pl.ANY),
        ],
        out_specs=pl.BlockSpec((tm, d), lambda i: (i, 0)),
        out_shape=jax.ShapeDtypeStruct((t, d), F32),
        scratch_shapes=[pltpu.VMEM((2, tm, ys.shape[1]), U32), pltpu.SemaphoreType.DMA(())],
        compiler_params=pltpu.CompilerParams(
            dimension_semantics=("arbitrary",), vmem_limit_bytes=_vmem_limit(est)),
        name="combine",
    )(pos3, x, info, ys)


def _ffn_moe_kernel(te_ref, tr_ref, xs_ref, wg_ref, wu_ref, wd_ref, y_ref, hn_ref, acc_ref):
    del te_ref
    i, f = pl.program_id(0), pl.program_id(1)
    rows = tr_ref[i]
    tm, d = acc_ref.shape
    half_rows = tm // 2

    @pl.when(f == 0)
    def _():
        acc_ref[...] = jnp.zeros_like(acc_ref)

    @pl.when(jnp.logical_and(rows > 0, f == 0))
    def _():
        hi, lo = _unpack_halves(xs_ref[...])
        hn_ref[:, :d // 2] = hi.astype(BF16)
        hn_ref[:, d // 2:] = lo.astype(BF16)

    @pl.when(rows > half_rows)
    def _():
        acc_ref[...] += _swiglu_step(hn_ref[...], wg_ref, wu_ref, wd_ref)

    @pl.when(jnp.logical_and(rows > 0, rows <= half_rows))
    def _():
        acc_ref[0:half_rows, :] += _swiglu_step(hn_ref[0:half_rows, :], wg_ref, wu_ref, wd_ref)

    @pl.when(f == pl.num_programs(1) - 1)
    def _():
        y_ref[...] = _pack_halves(acc_ref[...])


def _ffn_moe(xs, tile_expert, tile_rows, w_gu, w_down, tm=FFN_TM, tf=MOE_TF):
    n_slots, half = xs.shape
    d = 2 * half
    nf = D_FF // tf
    n_tiles = n_slots // tm
    wbytes = w_gu.dtype.itemsize

    def f_eff(i, f, tr):
        return jnp.where(tr[i] > 0, f, nf - 1)

    est = 4 * tm * half * 4 + tm * d * 2 + tm * d * 4 + 2 * 3 * d * tf * wbytes + 3 * d * tf * 2 + 3 * tm * tf * 4
    return pl.pallas_call(
        _ffn_moe_kernel,
        grid_spec=pltpu.PrefetchScalarGridSpec(
            num_scalar_prefetch=2,
            grid=(n_tiles, nf),
            in_specs=[
                pl.BlockSpec((tm, half), lambda i, f, te, tr: (i, 0)),
                pl.BlockSpec((None, d, tf), lambda i, f, te, tr: (te[i], 0, f_eff(i, f, tr))),
                pl.BlockSpec((None, d, tf), lambda i, f, te, tr: (te[i], 0, f_eff(i, f, tr) + nf)),
                pl.BlockSpec((None, tf, d), lambda i, f, te, tr: (te[i], f_eff(i, f, tr), 0)),
            ],
            out_specs=pl.BlockSpec((tm, half), lambda i, f, te, tr: (i, 0)),
            scratch_shapes=[pltpu.VMEM((tm, d), BF16), pltpu.VMEM((tm, d), F32)],
        ),
        out_shape=jax.ShapeDtypeStruct((n_slots, half), U32),
        compiler_params=pltpu.CompilerParams(
            dimension_semantics=("arbitrary", "arbitrary"), vmem_limit_bytes=_vmem_limit(est)),
        name="ffn_moe",
    )(tile_expert, tile_rows, xs, w_gu, w_gu, w_down)


def _moe_layer(x, gain, w_router, w_gu, w_down, tm=FFN_TM):
    t, _ = x.shape
    hp, info, cnt = _route(x, gain, w_router)
    counts = cnt[0, :N_EXPERTS].astype(I32)
    tiles_per_expert = (counts + tm - 1) // tm
    tile_end = jnp.cumsum(tiles_per_expert)
    group_start = (tile_end - tiles_per_expert) * tm
    n_tiles = (2 * t + N_EXPERTS * (tm - 1)) // tm
    tile_ids = jnp.arange(n_tiles, dtype=I32)
    tile_expert = jnp.minimum(jnp.sum(tile_ids[:, None] >= tile_end[None, :], axis=1), N_EXPERTS - 1).astype(I32)
    last_used_expert = tile_expert[jnp.maximum(tile_end[-1] - 1, 0)]
    used = tile_ids < tile_end[-1]
    tile_expert = jnp.where(used, tile_expert, last_used_expert)
    rows_left = counts[tile_expert] - (tile_ids * tm - group_start[tile_expert])
    tile_rows = jnp.where(used, jnp.clip(rows_left, 0, tm), 0).astype(I32)
    idx = info[:, 0:2].astype(I32)
    rank = info[:, 2:4].astype(I32)
    pos = (group_start[idx] + rank).astype(I32)

    xs = _scatter_rows(hp, pos, n_tiles * tm)
    ys = _ffn_moe(xs, tile_expert, tile_rows, w_gu, w_down)
    return _combine(x, info, ys, pos)


def _mem_kv(mem2d, norm_mem, w_mem_kv, k_gain):
    head_gain = jnp.concatenate([jnp.tile(k_gain, HEADS_PER_GROUP), jnp.ones((GROUP_WIDTH,), F32)])[None, :]
    segments = ((0, GROUP_WIDTH, True, 1), (GROUP_WIDTH, GROUP_WIDTH, False, 1))
    km, vm = _proj(mem2d, norm_mem, w_mem_kv.astype(BF16), head_gain, segments, seq=mem2d.shape[0])
    return km.reshape(-1, N_MEM, GROUP_WIDTH), vm.reshape(-1, N_MEM, GROUP_WIDTH)


def _dilated_mixer(x, mem2d, seq, norm_mix, norm_mem, w_in, qk_norm, w_mem_kv, w_out):
    n_grp = len(DILATIONS)
    head_gain = jnp.concatenate([
        jnp.tile(qk_norm[0], n_grp * HEADS_PER_GROUP), jnp.tile(qk_norm[1], n_grp * HEADS_PER_GROUP),
        jnp.ones((n_grp * GROUP_WIDTH,), F32), jnp.tile(qk_norm[2], HEADS_PER_GROUP)])[None, :]
    segments = tuple((c * GROUP_WIDTH, GROUP_WIDTH, c < 2 * n_grp, DILATIONS[c % n_grp]) for c in range(3 * n_grp))
    segments += ((3 * n_grp * GROUP_WIDTH, GROUP_WIDTH, True, 1),)
    *qkv, qm = _proj(x, norm_mix, w_in.astype(BF16), head_gain, segments, seq)
    km, vm = _mem_kv(mem2d, norm_mem, w_mem_kv, qk_norm[3])
    batch = x.shape[0] // seq
    attn = []
    for g, dilation in enumerate(DILATIONS):
        q, k, v = (a.reshape(batch, dilation, seq // dilation, GROUP_WIDTH) for a in qkv[g::n_grp])
        attn.append(_dil_attn(q, k, v))
    return _dil_out(x, attn, qm, km, vm, w_out.astype(BF16), seq)


def _pooling_mixer(x, mem2d, seq, norm_mix, norm_mem, w_in, w_pool, pool_scale, qk_norm, w_mem_kv, w_out):
    head_gain = jnp.concatenate([jnp.ones((POOL_WIDTH,), F32), jnp.tile(qk_norm[0], HEADS_PER_GROUP)])[None, :]
    segments = ((0, POOL_WIDTH, False, 1), (POOL_WIDTH, GROUP_WIDTH, True, 1))
    u, qm = _proj(x, norm_mix, w_in.astype(BF16), head_gain, segments, seq)
    km, vm = _mem_kv(mem2d, norm_mem, w_mem_kv, qk_norm[1])
    return _pool_out(x, u, qm, km, vm, w_pool.astype(BF16), pool_scale, w_out.astype(BF16), seq)


def kernel(x, mem, l0_norm_mix, l0_norm_mem, l0_w_in, l0_qk_norm, l0_w_mem_kv, l0_w_out, l0_norm_ffn, l0_w_gu, l0_w_down, l1_norm_mix, l1_norm_mem, l1_w_in, l1_w_pool, l1_pool_scale, l1_qk_norm, l1_w_mem_kv, l1_w_out, l1_norm_ffn, l1_w_router, l1_w_gu_e, l1_w_down_e, l2_norm_mix, l2_norm_mem, l2_w_in, l2_qk_norm, l2_w_mem_kv, l2_w_out, l2_norm_ffn, l2_w_gu, l2_w_down, l3_norm_mix, l3_norm_mem, l3_w_in, l3_w_pool, l3_pool_scale, l3_qk_norm, l3_w_mem_kv, l3_w_out, l3_norm_ffn, l3_w_router, l3_w_gu_e, l3_w_down_e):
    batch, seq, d = x.shape
    h = x.reshape(batch * seq, d)
    mem2d = mem.reshape(-1, d)

    h = _dilated_mixer(h, mem2d, seq, l0_norm_mix, l0_norm_mem, l0_w_in, l0_qk_norm, l0_w_mem_kv, l0_w_out)
    h = _ffn_dense(h, l0_norm_ffn, l0_w_gu.astype(BF16), l0_w_down.astype(BF16))
    h = _pooling_mixer(h, mem2d, seq, l1_norm_mix, l1_norm_mem, l1_w_in, l1_w_pool, l1_pool_scale,
                       l1_qk_norm, l1_w_mem_kv, l1_w_out)
    h = _moe_layer(h, l1_norm_ffn, l1_w_router, l1_w_gu_e, l1_w_down_e)
    h = _dilated_mixer(h, mem2d, seq, l2_norm_mix, l2_norm_mem, l2_w_in, l2_qk_norm, l2_w_mem_kv, l2_w_out)
    h = _ffn_dense(h, l2_norm_ffn, l2_w_gu.astype(BF16), l2_w_down.astype(BF16))
    h = _pooling_mixer(h, mem2d, seq, l3_norm_mix, l3_norm_mem, l3_w_in, l3_w_pool, l3_pool_scale,
                       l3_qk_norm, l3_w_mem_kv, l3_w_out)
    h = _moe_layer(h, l3_norm_ffn, l3_w_router, l3_w_gu_e, l3_w_down_e)
    return h.reshape(batch, seq, d)
```

```python
import functools

import jax
import jax.numpy as jnp
from jax import lax
from jax.experimental import pallas as pl
from jax.experimental.pallas import tpu as pltpu

F32 = jnp.float32
BF16 = jnp.bfloat16
U32 = jnp.uint32
I32 = jnp.int32

HEAD_DIM = 128
HEADS_PER_GROUP = 4
GROUP_WIDTH = HEADS_PER_GROUP * HEAD_DIM
DILATIONS = (1, 4, 16)
ATTN_BLOCK = 128
N_MEM = 256
POOL_WINDOWS = (2, 4, 8, 16)
POOL_GROUP = 384
POOL_WIDTH = 4 * POOL_GROUP
POOL_HALO = 16
D_FF = 7168
N_EXPERTS = 8
EPS = 1e-6
SCORE_SCALE = HEAD_DIM ** -0.5

LANES = 128
SUBLANES = 8
V7X_VMEM_BYTES = 64 * 1024 * 1024

PROJ_TM = 512
ATTN_QB = 512
OUT_TM = 512
FFN_TM, FFN_TF = 1024, 512
MOE_TM, MOE_SUB = 1536, 512
MOE_TF = 256
ROUTE_TM = 512
ROW_TM = 256
ROW_UNROLL = 8


def _vmem_limit(nbytes):
    return int(min(nbytes * 5 // 4 + (6 << 20), V7X_VMEM_BYTES - (4 << 20)))


def _rms(x, eps=EPS):
    return x * lax.rsqrt(jnp.mean(x * x, axis=-1, keepdims=True) + eps)


def _resident(shape):
    return pl.BlockSpec(shape, lambda *_: (0,) * len(shape), pipeline_mode=pl.Buffered(1))


def _proj_kernel(x_ref, g_ref, w_ref, hg_ref, *refs, segments):
    outs, (hn_ref, stage_ref) = refs[:len(segments)], refs[len(segments):]
    tm = x_ref.shape[0]
    hn_ref[...] = (_rms(x_ref[...]) * g_ref[...]).astype(BF16)
    for (col0, width, norm, dilation), o_ref in zip(segments, outs):
        for sub in range(width // GROUP_WIDTH):
            c0 = col0 + sub * GROUP_WIDTH
            acc = jnp.dot(hn_ref[...], w_ref[:, c0:c0 + GROUP_WIDTH], preferred_element_type=F32)
            for h in range(HEADS_PER_GROUP):
                cols = slice(h * HEAD_DIM, (h + 1) * HEAD_DIM)
                val = acc[:, cols]
                if norm:
                    val = _rms(val) * hg_ref[:, c0 + h * HEAD_DIM:c0 + (h + 1) * HEAD_DIM]
                if dilation == 1:
                    o_ref[:, sub * GROUP_WIDTH + h * HEAD_DIM:sub * GROUP_WIDTH + (h + 1) * HEAD_DIM] = val.astype(BF16)
                else:
                    stage_ref[h] = val
                    for r in range(dilation):
                        o_ref[r, :, cols] = stage_ref[h, pl.ds(r, tm // dilation, stride=dilation), :].astype(BF16)


def _proj(x, gain, w_bf16, head_gain, segments, seq, tm=PROJ_TM):
    t, d = x.shape
    tm = min(tm, t)
    tiles_per_seq = max(seq // tm, 1)
    out_specs, out_shapes = [], []
    for _, width, _, dilation in segments:
        if dilation == 1:
            out_specs.append(pl.BlockSpec((tm, width), lambda i: (i, 0)))
            out_shapes.append(jax.ShapeDtypeStruct((t, width), BF16))
        else:
            out_specs.append(pl.BlockSpec((None, dilation, tm // dilation, width),
                                          lambda i: (i // tiles_per_seq, 0, i % tiles_per_seq, 0)))
            out_shapes.append(jax.ShapeDtypeStruct((t // seq, dilation, seq // dilation, width), BF16))
    n_out_cols = sum(s[1] for s in segments)
    est = (2 * tm * d * 4 + tm * d * 2 + w_bf16.size * 2 + 4 * tm * n_out_cols + tm * GROUP_WIDTH * 4
           + 4 * tm * GROUP_WIDTH * 4)
    return pl.pallas_call(
        functools.partial(_proj_kernel, segments=tuple(segments)),
        grid=(t // tm,),
        in_specs=[
            pl.BlockSpec((tm, d), lambda i: (i, 0)),
            _resident((1, d)),
            _resident(w_bf16.shape),
            _resident(head_gain.shape),
        ],
        out_specs=out_specs,
        out_shape=out_shapes,
        scratch_shapes=[pltpu.VMEM((tm, d), BF16), pltpu.VMEM((HEADS_PER_GROUP, tm, HEAD_DIM), F32)],
        compiler_params=pltpu.CompilerParams(
            dimension_semantics=("arbitrary",), vmem_limit_bytes=_vmem_limit(est)),
        name="proj",
    )(x, gain.reshape(1, d), w_bf16, head_gain)


def _dil_attn_kernel(q_ref, kp_ref, k_ref, vp_ref, v_ref, o_ref, l_ref, *, qb):
    has_prev = pl.program_id(2) > 0
    qi = lax.broadcasted_iota(I32, (ATTN_BLOCK, 2 * ATTN_BLOCK), 0)
    kj = lax.broadcasted_iota(I32, (ATTN_BLOCK, 2 * ATTN_BLOCK), 1)
    in_prev = kj < ATTN_BLOCK
    kk = jnp.where(in_prev, kj, kj - ATTN_BLOCK)
    band = jnp.logical_and(kk >= jnp.where(in_prev, qi, 0), kk <= jnp.where(in_prev, ATTN_BLOCK, qi))
    band_first = jnp.logical_and(band, jnp.logical_or(has_prev, jnp.logical_not(in_prev)))
    for jb in range(qb // ATTN_BLOCK):
        rows = slice(jb * ATTN_BLOCK, (jb + 1) * ATTN_BLOCK)
        for h in range(HEADS_PER_GROUP):
            cols = slice(h * HEAD_DIM, (h + 1) * HEAD_DIM)
            q = q_ref[rows, cols]
            if jb == 0:
                keys = jnp.concatenate([kp_ref[:, cols], k_ref[rows, cols]], axis=0)
                vals = jnp.concatenate([vp_ref[:, cols], v_ref[rows, cols]], axis=0)
                mask = band_first
            else:
                both = slice((jb - 1) * ATTN_BLOCK, (jb + 1) * ATTN_BLOCK)
                keys = k_ref[both, cols]
                vals = v_ref[both, cols]
                mask = band
            s = lax.dot_general(q, keys, (((1,), (1,)), ((), ())), preferred_element_type=F32)
            s = jnp.where(mask, s * SCORE_SCALE, -jnp.inf)
            m = jnp.max(s, axis=-1, keepdims=True)
            p = jnp.exp(s - m)
            den = jnp.sum(p, axis=-1, keepdims=True)
            o = jnp.dot(p.astype(BF16), vals, preferred_element_type=F32)
            o_ref[rows, cols] = o / den
            l_ref[rows, cols] = jnp.broadcast_to(m + jnp.log(den), (ATTN_BLOCK, HEAD_DIM))


def _dil_attn(q, k, v, qb=ATTN_QB):
    batch, dilation, length, width = q.shape
    qb = min(qb, length)
    bpq = qb // ATTN_BLOCK
    cur = pl.BlockSpec((None, None, qb, width), lambda b, r, i: (b, r, i, 0))
    prev = pl.BlockSpec((None, None, ATTN_BLOCK, width), lambda b, r, i: (b, r, jnp.maximum(i * bpq - 1, 0), 0))
    out_sds = jax.ShapeDtypeStruct(q.shape, F32)
    est = 2 * (3 * qb + 2 * ATTN_BLOCK) * width * 2 + 2 * 2 * qb * width * 4
    return pl.pallas_call(
        functools.partial(_dil_attn_kernel, qb=qb),
        grid=(batch, dilation, length // qb),
        in_specs=[cur, prev, cur, prev, cur],
        out_specs=[cur, cur],
        out_shape=[out_sds, out_sds],
        compiler_params=pltpu.CompilerParams(
            dimension_semantics=("arbitrary", "arbitrary", "arbitrary"),
            vmem_limit_bytes=_vmem_limit(est)),
        name="dil_attn",
    )(q, k, k, v, v)


def _mem_attention(qm_ref, km_ref, vm_ref, cat_ref, col0):
    for h in range(HEADS_PER_GROUP):
        cols = slice(h * HEAD_DIM, (h + 1) * HEAD_DIM)
        s = lax.dot_general(qm_ref[:, cols], km_ref[:, cols], (((1,), (1,)), ((), ())),
                            preferred_element_type=F32) * SCORE_SCALE
        m = jnp.max(s, axis=-1, keepdims=True)
        p = jnp.exp(s - m)
        den = jnp.sum(p, axis=-1, keepdims=True)
        o = jnp.dot(p.astype(BF16), vm_ref[:, cols], preferred_element_type=F32)
        cat_ref[:, col0 + h * HEAD_DIM:col0 + (h + 1) * HEAD_DIM] = (o / den).astype(BF16)


def _to_token_order(src_ref, dst_ref):
    dilation, per_res, _ = src_ref.shape
    for h in range(HEADS_PER_GROUP):
        for r in range(dilation):
            dst_ref[h, pl.ds(r, per_res, stride=dilation), :] = src_ref[r, :, h * HEAD_DIM:(h + 1) * HEAD_DIM]


def _dil_out_kernel(o0_ref, l0_ref, o1_ref, l1_ref, o2_ref, l2_ref, qm_ref, km_ref, vm_ref,
                    x_ref, w_ref, y_ref, o1s, l1s, o2s, l2s, cat_ref):
    for src, dst in ((o1_ref, o1s), (l1_ref, l1s), (o2_ref, o2s), (l2_ref, l2s)):
        _to_token_order(src, dst)
    for h in range(HEADS_PER_GROUP):
        cols = slice(h * HEAD_DIM, (h + 1) * HEAD_DIM)
        l0, l1, l2 = l0_ref[:, cols], l1s[h], l2s[h]
        mx = jnp.maximum(jnp.maximum(l0, l1), l2)
        e0, e1, e2 = jnp.exp(l0 - mx), jnp.exp(l1 - mx), jnp.exp(l2 - mx)
        den = e0 + e1 + e2
        dil = (e0 / den) * o0_ref[:, cols] + (e1 / den) * o1s[h] + (e2 / den) * o2s[h]
        cat_ref[:, cols] = dil.astype(BF16)
    _mem_attention(qm_ref, km_ref, vm_ref, cat_ref, GROUP_WIDTH)
    y_ref[...] = x_ref[...] + jnp.dot(cat_ref[...], w_ref[...], preferred_element_type=F32)


def _dil_out(x, attn, qm, km, vm, w_out_bf16, seq, tm=OUT_TM):
    t, d = x.shape
    tiles_per_seq = seq // tm

    def grp(dilation):
        return pl.BlockSpec((None, dilation, tm // dilation, GROUP_WIDTH),
                            lambda i: (i // tiles_per_seq, 0, i % tiles_per_seq, 0))

    tok = pl.BlockSpec((tm, GROUP_WIDTH), lambda i: (i, 0))
    mem = pl.BlockSpec((None, N_MEM, GROUP_WIDTH), lambda i: (i // tiles_per_seq, 0, 0))
    (o0, l0), (o1, l1), (o2, l2) = attn
    stage = pltpu.VMEM((HEADS_PER_GROUP, tm, HEAD_DIM), F32)
    est = (2 * 6 * tm * GROUP_WIDTH * 4 + 4 * tm * GROUP_WIDTH * 4 + 2 * tm * GROUP_WIDTH * 2
           + 4 * N_MEM * GROUP_WIDTH * 2 + 4 * tm * d * 4 + w_out_bf16.size * 2 + 3 * tm * d * 4)
    return pl.pallas_call(
        _dil_out_kernel,
        grid=(t // tm,),
        in_specs=[tok, tok, grp(DILATIONS[1]), grp(DILATIONS[1]), grp(DILATIONS[2]), grp(DILATIONS[2]),
                  tok, mem, mem, pl.BlockSpec((tm, d), lambda i: (i, 0)), _resident(w_out_bf16.shape)],
        out_specs=pl.BlockSpec((tm, d), lambda i: (i, 0)),
        out_shape=jax.ShapeDtypeStruct((t, d), F32),
        scratch_shapes=[stage, stage, stage, stage, pltpu.VMEM((tm, w_out_bf16.shape[0]), BF16)],
        compiler_params=pltpu.CompilerParams(
            dimension_semantics=("arbitrary",), vmem_limit_bytes=_vmem_limit(est)),
        name="dil_out",
    )(o0.reshape(t, GROUP_WIDTH), l0.reshape(t, GROUP_WIDTH), o1, l1, o2, l2, qm, km, vm, x, w_out_bf16)


def _pool_out_kernel(u_ref, halo_ref, qm_ref, km_ref, vm_ref, x_ref, wp_ref, ps_ref, w_ref, y_ref, cat_ref,
                     *, tiles_per_seq):
    tm = u_ref.shape[0]
    tile_in_seq = pl.program_id(0) % tiles_per_seq
    halo = jnp.where(tile_in_seq > 0, halo_ref[...].astype(F32), 0.0)
    full = jnp.concatenate([halo, u_ref[...].astype(F32)], axis=0)
    pos = tile_in_seq * tm + lax.broadcasted_iota(I32, (tm, 1), 0)
    acc = full
    for g, window in enumerate(POOL_WINDOWS):
        lo = g * POOL_GROUP
        acc = acc[:, (POOL_GROUP if g else 0):]
        acc = acc + pltpu.roll(acc, window // 2, 0)
        count = jnp.minimum(pos + 1, window).astype(F32)
        pooled = acc[POOL_HALO:, :POOL_GROUP] / count - full[POOL_HALO:, lo:lo + POOL_GROUP]
        po = jnp.dot(pooled.astype(BF16), wp_ref[g], preferred_element_type=F32)
        cat_ref[:, lo:lo + POOL_GROUP] = (po * ps_ref[:, lo:lo + POOL_GROUP]).astype(BF16)
    _mem_attention(qm_ref, km_ref, vm_ref, cat_ref, POOL_WIDTH)
    y_ref[...] = x_ref[...] + jnp.dot(cat_ref[...], w_ref[...], preferred_element_type=F32)


def _pool_out(x, u, qm, km, vm, w_pool_bf16, pool_scale, w_out_bf16, seq, tm=OUT_TM):
    t, d = x.shape
    tiles_per_seq = seq // tm
    halo_blocks = tm // POOL_HALO
    mem = pl.BlockSpec((None, N_MEM, GROUP_WIDTH), lambda i: (i // tiles_per_seq, 0, 0))
    est = (2 * tm * d * 2 + 4 * N_MEM * GROUP_WIDTH * 2 + 4 * tm * d * 4 + w_out_bf16.size * 2
           + w_pool_bf16.size * 2 + 8 * tm * POOL_WIDTH * 4)
    return pl.pallas_call(
        functools.partial(_pool_out_kernel, tiles_per_seq=tiles_per_seq),
        grid=(t // tm,),
        in_specs=[
            pl.BlockSpec((tm, POOL_WIDTH), lambda i: (i, 0)),
            pl.BlockSpec((POOL_HALO, POOL_WIDTH), lambda i: (jnp.maximum(i * halo_blocks - 1, 0), 0)),
            pl.BlockSpec((tm, GROUP_WIDTH), lambda i: (i, 0)),
            mem, mem,
            pl.BlockSpec((tm, d), lambda i: (i, 0)),
            _resident(w_pool_bf16.shape),
            _resident((1, POOL_WIDTH)),
            _resident(w_out_bf16.shape),
        ],
        out_specs=pl.BlockSpec((tm, d), lambda i: (i, 0)),
        out_shape=jax.ShapeDtypeStruct((t, d), F32),
        scratch_shapes=[pltpu.VMEM((tm, w_out_bf16.shape[0]), BF16)],
        compiler_params=pltpu.CompilerParams(
            dimension_semantics=("arbitrary",), vmem_limit_bytes=_vmem_limit(est)),
        name="pool_out",
    )(u, u, qm, km, vm, x, w_pool_bf16, pool_scale.reshape(1, POOL_WIDTH), w_out_bf16)


def _swiglu_step(hn, wg_ref, wu_ref, wd_ref):
    gate = jnp.dot(hn, wg_ref[...].astype(BF16), preferred_element_type=F32)
    up = jnp.dot(hn, wu_ref[...].astype(BF16), preferred_element_type=F32)
    act = (gate * (1.0 / (1.0 + jnp.exp(-gate))) * up).astype(BF16)
    return jnp.dot(act, wd_ref[...].astype(BF16), preferred_element_type=F32)


def _ffn_dense_kernel(x_ref, g_ref, wg_ref, wu_ref, wd_ref, y_ref, hn_ref):
    @pl.when(pl.program_id(1) == 0)
    def _():
        x = x_ref[...]
        hn_ref[...] = (_rms(x) * g_ref[...]).astype(BF16)
        y_ref[...] = x

    y_ref[...] += _swiglu_step(hn_ref[...], wg_ref, wu_ref, wd_ref)


def _ffn_dense(x, gain, w_gu, w_down, tm=FFN_TM, tf=FFN_TF):
    t, d = x.shape
    nf = D_FF // tf
    wbytes = w_gu.dtype.itemsize
    est = 4 * tm * d * 4 + tm * d * 2 + 2 * 3 * d * tf * wbytes + 3 * tm * tf * 4
    return pl.pallas_call(
        _ffn_dense_kernel,
        grid=(t // tm, nf),
        in_specs=[
            pl.BlockSpec((tm, d), lambda i, f: (i, 0)),
            _resident((1, d)),
            pl.BlockSpec((d, tf), lambda i, f: (0, f)),
            pl.BlockSpec((d, tf), lambda i, f: (0, f + nf)),
            pl.BlockSpec((tf, d), lambda i, f: (f, 0)),
        ],
        out_specs=pl.BlockSpec((tm, d), lambda i, f: (i, 0)),
        out_shape=jax.ShapeDtypeStruct((t, d), F32),
        scratch_shapes=[pltpu.VMEM((tm, d), BF16)],
        compiler_params=pltpu.CompilerParams(
            dimension_semantics=("arbitrary", "arbitrary"), vmem_limit_bytes=_vmem_limit(est)),
        name="ffn_dense",
    )(x, gain.reshape(1, d), w_gu, w_gu, w_down)


def _route_kernel(x_ref, g_ref, whi_ref, wlo_ref, tri_ref, hp_ref, info_ref, cnt_ref, carry_ref):
    i = pl.program_id(0)

    @pl.when(i == 0)
    def _():
        carry_ref[...] = jnp.zeros_like(carry_ref)

    hn = _rms(x_ref[...]) * g_ref[...]
    hi = hn.astype(BF16)
    hi32 = hi.astype(F32)
    lo = (hn - hi32).astype(BF16)
    logits = (jnp.dot(hi, whi_ref[...], preferred_element_type=F32)
              + jnp.dot(hi, wlo_ref[...], preferred_element_type=F32)
              + jnp.dot(lo, whi_ref[...], preferred_element_type=F32))
    tm = logits.shape[0]
    lane = lax.broadcasted_iota(I32, (tm, LANES), 1).astype(F32)
    lg = jnp.where(lane < N_EXPERTS, logits, -jnp.inf)
    m1 = jnp.max(lg, axis=-1, keepdims=True)
    i1 = jnp.min(jnp.where(lg == m1, lane, float(LANES)), axis=-1, keepdims=True)
    lg2 = jnp.where(lane == i1, -jnp.inf, lg)
    m2 = jnp.max(lg2, axis=-1, keepdims=True)
    i2 = jnp.min(jnp.where(lg2 == m2, lane, float(LANES)), axis=-1, keepdims=True)
    e = jnp.exp(m2 - m1)
    g1 = 1.0 / (1.0 + e)
    g2 = e / (1.0 + e)
    sel1 = lane == i1
    sel2 = lane == i2
    onehot = jnp.where(jnp.logical_or(sel1, sel2), 1.0, 0.0)
    before = jnp.dot(tri_ref[...], onehot.astype(BF16), preferred_element_type=F32) + carry_ref[0:1, :]
    r1 = jnp.sum(jnp.where(sel1, before, 0.0), axis=-1, keepdims=True)
    r2 = jnp.sum(jnp.where(sel2, before, 0.0), axis=-1, keepdims=True)
    carry_ref[0:1, :] = carry_ref[0:1, :] + jnp.sum(onehot, axis=0, keepdims=True)
    cnt_ref[...] = carry_ref[...]

    info = jnp.where(lane == 0, i1, 0.0)
    info = jnp.where(lane == 1, i2, info)
    info = jnp.where(lane == 2, r1, info)
    info = jnp.where(lane == 3, r2, info)
    info = jnp.where(lane == 4, g1, info)
    info = jnp.where(lane == 5, g2, info)
    info_ref[...] = info

    hp_ref[...] = _pack_halves(hi32)


def _route(x, gain, w_router, tm=ROUTE_TM):
    t, d = x.shape
    wpad = jnp.zeros((d, LANES), F32).at[:, :N_EXPERTS].set(w_router)
    whi = wpad.astype(BF16)
    wlo = (wpad - whi.astype(F32)).astype(BF16)
    tri = (lax.broadcasted_iota(I32, (tm, tm), 0) > lax.broadcasted_iota(I32, (tm, tm), 1)).astype(BF16)
    est = 4 * tm * d * 4 + 2 * tm * d * 2 + 2 * d * LANES * 2 + tm * tm * 2 + 8 * tm * LANES * 4
    return pl.pallas_call(
        _route_kernel,
        grid=(t // tm,),
        in_specs=[
            pl.BlockSpec((tm, d), lambda i: (i, 0)),
            _resident((1, d)),
            _resident((d, LANES)),
            _resident((d, LANES)),
            _resident((tm, tm)),
        ],
        out_specs=[
            pl.BlockSpec((tm, d // 2), lambda i: (i, 0)),
            pl.BlockSpec((tm, LANES), lambda i: (i, 0)),
            pl.BlockSpec((SUBLANES, LANES), lambda i: (0, 0)),
        ],
        out_shape=[
            jax.ShapeDtypeStruct((t, d // 2), U32),
            jax.ShapeDtypeStruct((t, LANES), F32),
            jax.ShapeDtypeStruct((SUBLANES, LANES), F32),
        ],
        scratch_shapes=[pltpu.VMEM((SUBLANES, LANES), F32)],
        compiler_params=pltpu.CompilerParams(
            dimension_semantics=("arbitrary",), vmem_limit_bytes=_vmem_limit(est)),
        name="route",
    )(x, gain.reshape(1, d), whi, wlo, tri)


def _pack_halves(x):
    bits = pltpu.bitcast(x.astype(BF16).astype(F32), U32)
    half = bits.shape[1] // 2
    return (bits[:, :half] & jnp.uint32(0xFFFF0000)) | (bits[:, half:] >> 16)


def _unpack_halves(words):
    return pltpu.bitcast(words & jnp.uint32(0xFFFF0000), F32), pltpu.bitcast(words << 16, F32)


def _for_each_row(tm, fn):
    def body(c, carry):
        for u in range(ROW_UNROLL):
            fn(c * ROW_UNROLL + u)
        return carry

    lax.fori_loop(0, tm // ROW_UNROLL, body, 0)


def _scatter_kernel(pos_ref, hp_ref, xs_hbm_in, xs_hbm, sem):
    del xs_hbm_in
    tm = hp_ref.shape[0]

    def row_copy(r, k):
        return pltpu.make_async_copy(hp_ref.at[pl.ds(r, 1)], xs_hbm.at[pl.ds(pos_ref[0, 2 * r + k], 1)], sem)

    def start(r):
        row_copy(r, 0).start()
        row_copy(r, 1).start()

    def wait(r):
        row_copy(r, 0).wait()
        row_copy(r, 1).wait()

    _for_each_row(tm, start)
    _for_each_row(tm, wait)


def _scatter_rows(hp, pos, n_slots, tm=ROW_TM):
    t = hp.shape[0]
    pos3 = pos.reshape(t // tm, 1, 2 * tm)
    return pl.pallas_call(
        _scatter_kernel,
        grid=(t // tm,),
        in_specs=[
            pl.BlockSpec((None, 1, 2 * tm), lambda i: (i, 0, 0), memory_space=pltpu.SMEM),
            pl.BlockSpec((tm, hp.shape[1]), lambda i: (i, 0)),
            pl.BlockSpec(memory_space=pl.ANY),
        ],
        out_specs=pl.BlockSpec(memory_space=pl.ANY),
        out_shape=jax.ShapeDtypeStruct((n_slots, hp.shape[1]), U32),
        scratch_shapes=[pltpu.SemaphoreType.DMA(())],
        input_output_aliases={2: 0},
        compiler_params=pltpu.CompilerParams(dimension_semantics=("arbitrary",)),
        name="scatter_rows",
    )(pos3, hp, jnp.zeros((n_slots, hp.shape[1]), U32))


def _combine_kernel(pos_ref, x_ref, info_ref, ys_hbm, y_ref, buf_ref, sem):
    tm, d = x_ref.shape

    def row_copy(r, k):
        return pltpu.make_async_copy(ys_hbm.at[pl.ds(pos_ref[0, 2 * r + k], 1)], buf_ref.at[k, pl.ds(r, 1)], sem)

    def start(r):
        row_copy(r, 0).start()
        row_copy(r, 1).start()

    def wait(r):
        row_copy(r, 0).wait()
        row_copy(r, 1).wait()

    _for_each_row(tm, start)
    _for_each_row(tm, wait)
    info = info_ref[...]
    g1, g2 = info[:, 4:5], info[:, 5:6]
    a_hi, a_lo = _unpack_halves(buf_ref[0])
    b_hi, b_lo = _unpack_halves(buf_ref[1])
    y_ref[:, :d // 2] = x_ref[:, :d // 2] + g1 * a_hi + g2 * b_hi
    y_ref[:, d // 2:] = x_ref[:, d // 2:] + g1 * a_lo + g2 * b_lo


def _combine(x, info, ys, pos, tm=ROW_TM):
    t, d = x.shape
    pos3 = pos.reshape(t // tm, 1, 2 * tm)
    est = 4 * tm * d * 4 + 2 * tm * LANES * 4 + 2 * tm * d * 2 + 2 * tm * d * 4
    return pl.pallas_call(
        _combine_kernel,
        grid=(t // tm,),
        in_specs=[
            pl.BlockSpec((None, 1, 2 * tm), lambda i: (i, 0, 0), memory_space=pltpu.SMEM),
            pl.BlockSpec((tm, d), lambda i: (i, 0)),
            pl.BlockSpec((tm, LANES), lambda i: (i, 0)),
            pl.BlockSpec(memory_space=pl.ANY),
        ],
        out_specs=pl.BlockSpec((tm, d), lambda i: (i, 0)),
        out_shape=jax.ShapeDtypeStruct((t, d), F32),
        scratch_shapes=[pltpu.VMEM((2, tm, ys.shape[1]), U32), pltpu.SemaphoreType.DMA(())],
        compiler_params=pltpu.CompilerParams(
            dimension_semantics=("arbitrary",), vmem_limit_bytes=_vmem_limit(est)),
        name="combine",
    )(pos3, x, info, ys)


def _ffn_moe_kernel(te_ref, tr_ref, xs_ref, wg_ref, wu_ref, wd_ref, y_ref, hn_ref, acc_ref):
    del te_ref
    i, f = pl.program_id(0), pl.program_id(1)
    rows = tr_ref[i]
    tm, d = acc_ref.shape

    @pl.when(f == 0)
    def _():
        acc_ref[...] = jnp.zeros_like(acc_ref)

    @pl.when(jnp.logical_and(rows > 0, f == 0))
    def _():
        hi, lo = _unpack_halves(xs_ref[...])
        hn_ref[:, :d // 2] = hi.astype(BF16)
        hn_ref[:, d // 2:] = lo.astype(BF16)

    for n_sub in range(1, tm // MOE_SUB + 1):
        live = n_sub * MOE_SUB
        in_range = jnp.logical_and(rows > live - MOE_SUB, rows <= live)

        @pl.when(in_range)
        def _():
            acc_ref[0:live, :] += _swiglu_step(hn_ref[0:live, :], wg_ref, wu_ref, wd_ref)

    @pl.when(f == pl.num_programs(1) - 1)
    def _():
        y_ref[...] = _pack_halves(acc_ref[...])


def _ffn_moe(xs, tile_expert, tile_rows, w_gu, w_down, tm=MOE_TM, tf=MOE_TF):
    n_slots, half = xs.shape
    d = 2 * half
    nf = D_FF // tf
    n_tiles = n_slots // tm
    wbytes = w_gu.dtype.itemsize

    def f_eff(i, f, tr):
        return jnp.where(tr[i] > 0, f, nf - 1)

    est = 4 * tm * half * 4 + tm * d * 2 + tm * d * 4 + 2 * 3 * d * tf * wbytes + 3 * d * tf * 2 + 3 * tm * tf * 4
    return pl.pallas_call(
        _ffn_moe_kernel,
        grid_spec=pltpu.PrefetchScalarGridSpec(
            num_scalar_prefetch=2,
            grid=(n_tiles, nf),
            in_specs=[
                pl.BlockSpec((tm, half), lambda i, f, te, tr: (i, 0)),
                pl.BlockSpec((None, d, tf), lambda i, f, te, tr: (te[i], 0, f_eff(i, f, tr))),
                pl.BlockSpec((None, d, tf), lambda i, f, te, tr: (te[i], 0, f_eff(i, f, tr) + nf)),
                pl.BlockSpec((None, tf, d), lambda i, f, te, tr: (te[i], f_eff(i, f, tr), 0)),
            ],
            out_specs=pl.BlockSpec((tm, half), lambda i, f, te, tr: (i, 0)),
            scratch_shapes=[pltpu.VMEM((tm, d), BF16), pltpu.VMEM((tm, d), F32)],
        ),
        out_shape=jax.ShapeDtypeStruct((n_slots, half), U32),
        compiler_params=pltpu.CompilerParams(
            dimension_semantics=("arbitrary", "arbitrary"), vmem_limit_bytes=_vmem_limit(est)),
        name="ffn_moe",
    )(tile_expert, tile_rows, xs, w_gu, w_gu, w_down)


def _moe_layer(x, gain, w_router, w_gu, w_down, tm=MOE_TM):
    t, _ = x.shape
    hp, info, cnt = _route(x, gain, w_router)
    counts = cnt[0, :N_EXPERTS].astype(I32)
    tiles_per_expert = (counts + tm - 1) // tm
    tile_end = jnp.cumsum(tiles_per_expert)
    group_start = (tile_end - tiles_per_expert) * tm
    n_tiles = (2 * t + N_EXPERTS * (tm - 1)) // tm
    tile_ids = jnp.arange(n_tiles, dtype=I32)
    tile_expert = jnp.minimum(jnp.sum(tile_ids[:, None] >= tile_end[None, :], axis=1), N_EXPERTS - 1).astype(I32)
    last_used_expert = tile_expert[jnp.maximum(tile_end[-1] - 1, 0)]
    used = tile_ids < tile_end[-1]
    tile_expert = jnp.where(used, tile_expert, last_used_expert)
    rows_left = counts[tile_expert] - (tile_ids * tm - group_start[tile_expert])
    tile_rows = jnp.where(used, jnp.clip(rows_left, 0, tm), 0).astype(I32)
    idx = info[:, 0:2].astype(I32)
    rank = info[:, 2:4].astype(I32)
    pos = (group_start[idx] + rank).astype(I32)

    xs = _scatter_rows(hp, pos, n_tiles * tm)
    ys = _ffn_moe(xs, tile_expert, tile_rows, w_gu, w_down)
    return _combine(x, info, ys, pos)


def _mem_kv(mem2d, norm_mem, w_mem_kv, k_gain):
    head_gain = jnp.concatenate([jnp.tile(k_gain, HEADS_PER_GROUP), jnp.ones((GROUP_WIDTH,), F32)])[None, :]
    segments = ((0, GROUP_WIDTH, True, 1), (GROUP_WIDTH, GROUP_WIDTH, False, 1))
    km, vm = _proj(mem2d, norm_mem, w_mem_kv.astype(BF16), head_gain, segments, seq=mem2d.shape[0])
    return km.reshape(-1, N_MEM, GROUP_WIDTH), vm.reshape(-1, N_MEM, GROUP_WIDTH)


def _dilated_mixer(x, mem2d, seq, norm_mix, norm_mem, w_in, qk_norm, w_mem_kv, w_out):
    n_grp = len(DILATIONS)
    head_gain = jnp.concatenate([
        jnp.tile(qk_norm[0], n_grp * HEADS_PER_GROUP), jnp.tile(qk_norm[1], n_grp * HEADS_PER_GROUP),
        jnp.ones((n_grp * GROUP_WIDTH,), F32), jnp.tile(qk_norm[2], HEADS_PER_GROUP)])[None, :]
    segments = tuple((c * GROUP_WIDTH, GROUP_WIDTH, c < 2 * n_grp, DILATIONS[c % n_grp]) for c in range(3 * n_grp))
    segments += ((3 * n_grp * GROUP_WIDTH, GROUP_WIDTH, True, 1),)
    *qkv, qm = _proj(x, norm_mix, w_in.astype(BF16), head_gain, segments, seq)
    km, vm = _mem_kv(mem2d, norm_mem, w_mem_kv, qk_norm[3])
    batch = x.shape[0] // seq
    attn = []
    for g, dilation in enumerate(DILATIONS):
        q, k, v = (a.reshape(batch, dilation, seq // dilation, GROUP_WIDTH) for a in qkv[g::n_grp])
        attn.append(_dil_attn(q, k, v))
    return _dil_out(x, attn, qm, km, vm, w_out.astype(BF16), seq)


def _pooling_mixer(x, mem2d, seq, norm_mix, norm_mem, w_in, w_pool, pool_scale, qk_norm, w_mem_kv, w_out):
    head_gain = jnp.concatenate([jnp.ones((POOL_WIDTH,), F32), jnp.tile(qk_norm[0], HEADS_PER_GROUP)])[None, :]
    segments = ((0, POOL_WIDTH, False, 1), (POOL_WIDTH, GROUP_WIDTH, True, 1))
    u, qm = _proj(x, norm_mix, w_in.astype(BF16), head_gain, segments, seq)
    km, vm = _mem_kv(mem2d, norm_mem, w_mem_kv, qk_norm[1])
    return _pool_out(x, u, qm, km, vm, w_pool.astype(BF16), pool_scale, w_out.astype(BF16), seq)


def kernel(x, mem, l0_norm_mix, l0_norm_mem, l0_w_in, l0_qk_norm, l0_w_mem_kv, l0_w_out, l0_norm_ffn, l0_w_gu, l0_w_down, l1_norm_mix, l1_norm_mem, l1_w_in, l1_w_pool, l1_pool_scale, l1_qk_norm, l1_w_mem_kv, l1_w_out, l1_norm_ffn, l1_w_router, l1_w_gu_e, l1_w_down_e, l2_norm_mix, l2_norm_mem, l2_w_in, l2_qk_norm, l2_w_mem_kv, l2_w_out, l2_norm_ffn, l2_w_gu, l2_w_down, l3_norm_mix, l3_norm_mem, l3_w_in, l3_w_pool, l3_pool_scale, l3_qk_norm, l3_w_mem_kv, l3_w_out, l3_norm_ffn, l3_w_router, l3_w_gu_e, l3_w_down_e):
    batch, seq, d = x.shape
    h = x.reshape(batch * seq, d)
    mem2d = mem.reshape(-1, d)

    h = _dilated_mixer(h, mem2d, seq, l0_norm_mix, l0_norm_mem, l0_w_in, l0_qk_norm, l0_w_mem_kv, l0_w_out)
    h = _ffn_dense(h, l0_norm_ffn, l0_w_gu.astype(BF16), l0_w_down.astype(BF16))
    h = _pooling_mixer(h, mem2d, seq, l1_norm_mix, l1_norm_mem, l1_w_in, l1_w_pool, l1_pool_scale,
                       l1_qk_norm, l1_w_mem_kv, l1_w_out)
    h = _moe_layer(h, l1_norm_ffn, l1_w_router, l1_w_gu_e, l1_w_down_e)
    h = _dilated_mixer(h, mem2d, seq, l2_norm_mix, l2_norm_mem, l2_w_in, l2_qk_norm, l2_w_mem_kv, l2_w_out)
    h = _ffn_dense(h, l2_norm_ffn, l2_w_gu.astype(BF16), l2_w_down.astype(BF16))
    h = _pooling_mixer(h, mem2d, seq, l3_norm_mix, l3_norm_mem, l3_w_in, l3_w_pool, l3_pool_scale,
                       l3_qk_norm, l3_w_mem_kv, l3_w_out)
    h = _moe_layer(h, l3_norm_ffn, l3_w_router, l3_w_gu_e, l3_w_down_e)
    return h.reshape(batch, seq, d)
```

```python
import functools

import jax
import jax.numpy as jnp
from jax import lax
from jax.experimental import pallas as pl
from jax.experimental.pallas import tpu as pltpu

F32 = jnp.float32
BF16 = jnp.bfloat16
U32 = jnp.uint32
I32 = jnp.int32

HEAD_DIM = 128
HEADS_PER_GROUP = 4
GROUP_WIDTH = HEADS_PER_GROUP * HEAD_DIM
DILATIONS = (1, 4, 16)
ATTN_BLOCK = 128
N_MEM = 256
POOL_WINDOWS = (2, 4, 8, 16)
POOL_GROUP = 384
POOL_WIDTH = 4 * POOL_GROUP
POOL_HALO = 16
D_FF = 7168
N_EXPERTS = 8
EPS = 1e-6
SCORE_SCALE = HEAD_DIM ** -0.5

LANES = 128
SUBLANES = 8
V7X_VMEM_BYTES = 64 * 1024 * 1024

PROJ_TM = 512
ATTN_QB = 512
OUT_TM = 512
FFN_TM, FFN_TF = 1024, 512
MOE_TM, MOE_SUB = 1536, 512
MOE_TF = 256
ROUTE_TM = 512
ROW_TM = 1024
ROW_UNROLL = 8


def _vmem_limit(nbytes):
    return int(min(nbytes * 5 // 4 + (6 << 20), V7X_VMEM_BYTES - (4 << 20)))


def _rms(x, eps=EPS):
    return x * lax.rsqrt(jnp.mean(x * x, axis=-1, keepdims=True) + eps)


def _resident(shape):
    return pl.BlockSpec(shape, lambda *_: (0,) * len(shape), pipeline_mode=pl.Buffered(1))


def _proj_kernel(x_ref, g_ref, w_ref, hg_ref, *refs, segments):
    outs, (hn_ref, stage_ref) = refs[:len(segments)], refs[len(segments):]
    tm = x_ref.shape[0]
    hn_ref[...] = (_rms(x_ref[...]) * g_ref[...]).astype(BF16)
    for (col0, width, norm, dilation), o_ref in zip(segments, outs):
        for sub in range(width // GROUP_WIDTH):
            c0 = col0 + sub * GROUP_WIDTH
            acc = jnp.dot(hn_ref[...], w_ref[:, c0:c0 + GROUP_WIDTH], preferred_element_type=F32)
            for h in range(HEADS_PER_GROUP):
                cols = slice(h * HEAD_DIM, (h + 1) * HEAD_DIM)
                val = acc[:, cols]
                if norm:
                    val = _rms(val) * hg_ref[:, c0 + h * HEAD_DIM:c0 + (h + 1) * HEAD_DIM]
                if dilation == 1:
                    o_ref[:, sub * GROUP_WIDTH + h * HEAD_DIM:sub * GROUP_WIDTH + (h + 1) * HEAD_DIM] = val.astype(BF16)
                else:
                    stage_ref[h] = val
                    for r in range(dilation):
                        o_ref[r, :, cols] = stage_ref[h, pl.ds(r, tm // dilation, stride=dilation), :].astype(BF16)


def _proj(x, gain, w_bf16, head_gain, segments, seq, tm=PROJ_TM):
    t, d = x.shape
    tm = min(tm, t)
    tiles_per_seq = max(seq // tm, 1)
    out_specs, out_shapes = [], []
    for _, width, _, dilation in segments:
        if dilation == 1:
            out_specs.append(pl.BlockSpec((tm, width), lambda i: (i, 0)))
            out_shapes.append(jax.ShapeDtypeStruct((t, width), BF16))
        else:
            out_specs.append(pl.BlockSpec((None, dilation, tm // dilation, width),
                                          lambda i: (i // tiles_per_seq, 0, i % tiles_per_seq, 0)))
            out_shapes.append(jax.ShapeDtypeStruct((t // seq, dilation, seq // dilation, width), BF16))
    n_out_cols = sum(s[1] for s in segments)
    est = (2 * tm * d * 4 + tm * d * 2 + w_bf16.size * 2 + 4 * tm * n_out_cols + tm * GROUP_WIDTH * 4
           + 4 * tm * GROUP_WIDTH * 4)
    return pl.pallas_call(
        functools.partial(_proj_kernel, segments=tuple(segments)),
        grid=(t // tm,),
        in_specs=[
            pl.BlockSpec((tm, d), lambda i: (i, 0)),
            _resident((1, d)),
            _resident(w_bf16.shape),
            _resident(head_gain.shape),
        ],
        out_specs=out_specs,
        out_shape=out_shapes,
        scratch_shapes=[pltpu.VMEM((tm, d), BF16), pltpu.VMEM((HEADS_PER_GROUP, tm, HEAD_DIM), F32)],
        compiler_params=pltpu.CompilerParams(
            dimension_semantics=("arbitrary",), vmem_limit_bytes=_vmem_limit(est)),
        name="proj",
    )(x, gain.reshape(1, d), w_bf16, head_gain)


def _dil_attn_kernel(q_ref, kp_ref, k_ref, vp_ref, v_ref, o_ref, l_ref, *, qb):
    has_prev = pl.program_id(2) > 0
    qi = lax.broadcasted_iota(I32, (ATTN_BLOCK, 2 * ATTN_BLOCK), 0)
    kj = lax.broadcasted_iota(I32, (ATTN_BLOCK, 2 * ATTN_BLOCK), 1)
    in_prev = kj < ATTN_BLOCK
    kk = jnp.where(in_prev, kj, kj - ATTN_BLOCK)
    band = jnp.logical_and(kk >= jnp.where(in_prev, qi, 0), kk <= jnp.where(in_prev, ATTN_BLOCK, qi))
    band_first = jnp.logical_and(band, jnp.logical_or(has_prev, jnp.logical_not(in_prev)))
    for jb in range(qb // ATTN_BLOCK):
        rows = slice(jb * ATTN_BLOCK, (jb + 1) * ATTN_BLOCK)
        for h in range(HEADS_PER_GROUP):
            cols = slice(h * HEAD_DIM, (h + 1) * HEAD_DIM)
            q = q_ref[rows, cols]
            if jb == 0:
                keys = jnp.concatenate([kp_ref[:, cols], k_ref[rows, cols]], axis=0)
                vals = jnp.concatenate([vp_ref[:, cols], v_ref[rows, cols]], axis=0)
                mask = band_first
            else:
                both = slice((jb - 1) * ATTN_BLOCK, (jb + 1) * ATTN_BLOCK)
                keys = k_ref[both, cols]
                vals = v_ref[both, cols]
                mask = band
            s = lax.dot_general(q, keys, (((1,), (1,)), ((), ())), preferred_element_type=F32)
            s = jnp.where(mask, s * SCORE_SCALE, -jnp.inf)
            m = jnp.max(s, axis=-1, keepdims=True)
            p = jnp.exp(s - m)
            den = jnp.sum(p, axis=-1, keepdims=True)
            o = jnp.dot(p.astype(BF16), vals, preferred_element_type=F32)
            o_ref[rows, cols] = o / den
            l_ref[rows, cols] = jnp.broadcast_to(m + jnp.log(den), (ATTN_BLOCK, HEAD_DIM))


def _dil_attn(q, k, v, qb=ATTN_QB):
    batch, dilation, length, width = q.shape
    qb = min(qb, length)
    bpq = qb // ATTN_BLOCK
    cur = pl.BlockSpec((None, None, qb, width), lambda b, r, i: (b, r, i, 0))
    prev = pl.BlockSpec((None, None, ATTN_BLOCK, width), lambda b, r, i: (b, r, jnp.maximum(i * bpq - 1, 0), 0))
    out_sds = jax.ShapeDtypeStruct(q.shape, F32)
    est = 2 * (3 * qb + 2 * ATTN_BLOCK) * width * 2 + 2 * 2 * qb * width * 4
    return pl.pallas_call(
        functools.partial(_dil_attn_kernel, qb=qb),
        grid=(batch, dilation, length // qb),
        in_specs=[cur, prev, cur, prev, cur],
        out_specs=[cur, cur],
        out_shape=[out_sds, out_sds],
        compiler_params=pltpu.CompilerParams(
            dimension_semantics=("arbitrary", "arbitrary", "arbitrary"),
            vmem_limit_bytes=_vmem_limit(est)),
        name="dil_attn",
    )(q, k, k, v, v)


def _mem_attention(qm_ref, km_ref, vm_ref, cat_ref, col0):
    for h in range(HEADS_PER_GROUP):
        cols = slice(h * HEAD_DIM, (h + 1) * HEAD_DIM)
        s = lax.dot_general(qm_ref[:, cols], km_ref[:, cols], (((1,), (1,)), ((), ())),
                            preferred_element_type=F32) * SCORE_SCALE
        m = jnp.max(s, axis=-1, keepdims=True)
        p = jnp.exp(s - m)
        den = jnp.sum(p, axis=-1, keepdims=True)
        o = jnp.dot(p.astype(BF16), vm_ref[:, cols], preferred_element_type=F32)
        cat_ref[:, col0 + h * HEAD_DIM:col0 + (h + 1) * HEAD_DIM] = (o / den).astype(BF16)


def _to_token_order(src_ref, dst_ref):
    dilation, per_res, _ = src_ref.shape
    for h in range(HEADS_PER_GROUP):
        for r in range(dilation):
            dst_ref[h, pl.ds(r, per_res, stride=dilation), :] = src_ref[r, :, h * HEAD_DIM:(h + 1) * HEAD_DIM]


def _dil_out_kernel(o0_ref, l0_ref, o1_ref, l1_ref, o2_ref, l2_ref, qm_ref, km_ref, vm_ref,
                    x_ref, w_ref, y_ref, o1s, l1s, o2s, l2s, cat_ref):
    for src, dst in ((o1_ref, o1s), (l1_ref, l1s), (o2_ref, o2s), (l2_ref, l2s)):
        _to_token_order(src, dst)
    for h in range(HEADS_PER_GROUP):
        cols = slice(h * HEAD_DIM, (h + 1) * HEAD_DIM)
        l0, l1, l2 = l0_ref[:, cols], l1s[h], l2s[h]
        mx = jnp.maximum(jnp.maximum(l0, l1), l2)
        e0, e1, e2 = jnp.exp(l0 - mx), jnp.exp(l1 - mx), jnp.exp(l2 - mx)
        den = e0 + e1 + e2
        dil = (e0 / den) * o0_ref[:, cols] + (e1 / den) * o1s[h] + (e2 / den) * o2s[h]
        cat_ref[:, cols] = dil.astype(BF16)
    _mem_attention(qm_ref, km_ref, vm_ref, cat_ref, GROUP_WIDTH)
    y_ref[...] = x_ref[...] + jnp.dot(cat_ref[...], w_ref[...], preferred_element_type=F32)


def _dil_out(x, attn, qm, km, vm, w_out_bf16, seq, tm=OUT_TM):
    t, d = x.shape
    tiles_per_seq = seq // tm

    def grp(dilation):
        return pl.BlockSpec((None, dilation, tm // dilation, GROUP_WIDTH),
                            lambda i: (i // tiles_per_seq, 0, i % tiles_per_seq, 0))

    tok = pl.BlockSpec((tm, GROUP_WIDTH), lambda i: (i, 0))
    mem = pl.BlockSpec((None, N_MEM, GROUP_WIDTH), lambda i: (i // tiles_per_seq, 0, 0))
    (o0, l0), (o1, l1), (o2, l2) = attn
    stage = pltpu.VMEM((HEADS_PER_GROUP, tm, HEAD_DIM), F32)
    est = (2 * 6 * tm * GROUP_WIDTH * 4 + 4 * tm * GROUP_WIDTH * 4 + 2 * tm * GROUP_WIDTH * 2
           + 4 * N_MEM * GROUP_WIDTH * 2 + 4 * tm * d * 4 + w_out_bf16.size * 2 + 3 * tm * d * 4)
    return pl.pallas_call(
        _dil_out_kernel,
        grid=(t // tm,),
        in_specs=[tok, tok, grp(DILATIONS[1]), grp(DILATIONS[1]), grp(DILATIONS[2]), grp(DILATIONS[2]),
                  tok, mem, mem, pl.BlockSpec((tm, d), lambda i: (i, 0)), _resident(w_out_bf16.shape)],
        out_specs=pl.BlockSpec((tm, d), lambda i: (i, 0)),
        out_shape=jax.ShapeDtypeStruct((t, d), F32),
        scratch_shapes=[stage, stage, stage, stage, pltpu.VMEM((tm, w_out_bf16.shape[0]), BF16)],
        compiler_params=pltpu.CompilerParams(
            dimension_semantics=("arbitrary",), vmem_limit_bytes=_vmem_limit(est)),
        name="dil_out",
    )(o0.reshape(t, GROUP_WIDTH), l0.reshape(t, GROUP_WIDTH), o1, l1, o2, l2, qm, km, vm, x, w_out_bf16)


def _pool_out_kernel(u_ref, halo_ref, qm_ref, km_ref, vm_ref, x_ref, wp_ref, ps_ref, w_ref, y_ref, cat_ref,
                     *, tiles_per_seq):
    tm = u_ref.shape[0]
    tile_in_seq = pl.program_id(0) % tiles_per_seq
    halo = jnp.where(tile_in_seq > 0, halo_ref[...].astype(F32), 0.0)
    full = jnp.concatenate([halo, u_ref[...].astype(F32)], axis=0)
    pos = tile_in_seq * tm + lax.broadcasted_iota(I32, (tm, 1), 0)
    acc = full
    for g, window in enumerate(POOL_WINDOWS):
        lo = g * POOL_GROUP
        acc = acc[:, (POOL_GROUP if g else 0):]
        acc = acc + pltpu.roll(acc, window // 2, 0)
        count = jnp.minimum(pos + 1, window).astype(F32)
        pooled = acc[POOL_HALO:, :POOL_GROUP] / count - full[POOL_HALO:, lo:lo + POOL_GROUP]
        po = jnp.dot(pooled.astype(BF16), wp_ref[g], preferred_element_type=F32)
        cat_ref[:, lo:lo + POOL_GROUP] = (po * ps_ref[:, lo:lo + POOL_GROUP]).astype(BF16)
    _mem_attention(qm_ref, km_ref, vm_ref, cat_ref, POOL_WIDTH)
    y_ref[...] = x_ref[...] + jnp.dot(cat_ref[...], w_ref[...], preferred_element_type=F32)


def _pool_out(x, u, qm, km, vm, w_pool_bf16, pool_scale, w_out_bf16, seq, tm=OUT_TM):
    t, d = x.shape
    tiles_per_seq = seq // tm
    halo_blocks = tm // POOL_HALO
    mem = pl.BlockSpec((None, N_MEM, GROUP_WIDTH), lambda i: (i // tiles_per_seq, 0, 0))
    est = (2 * tm * d * 2 + 4 * N_MEM * GROUP_WIDTH * 2 + 4 * tm * d * 4 + w_out_bf16.size * 2
           + w_pool_bf16.size * 2 + 8 * tm * POOL_WIDTH * 4)
    return pl.pallas_call(
        functools.partial(_pool_out_kernel, tiles_per_seq=tiles_per_seq),
        grid=(t // tm,),
        in_specs=[
            pl.BlockSpec((tm, POOL_WIDTH), lambda i: (i, 0)),
            pl.BlockSpec((POOL_HALO, POOL_WIDTH), lambda i: (jnp.maximum(i * halo_blocks - 1, 0), 0)),
            pl.BlockSpec((tm, GROUP_WIDTH), lambda i: (i, 0)),
            mem, mem,
            pl.BlockSpec((tm, d), lambda i: (i, 0)),
            _resident(w_pool_bf16.shape),
            _resident((1, POOL_WIDTH)),
            _resident(w_out_bf16.shape),
        ],
        out_specs=pl.BlockSpec((tm, d), lambda i: (i, 0)),
        out_shape=jax.ShapeDtypeStruct((t, d), F32),
        scratch_shapes=[pltpu.VMEM((tm, w_out_bf16.shape[0]), BF16)],
        compiler_params=pltpu.CompilerParams(
            dimension_semantics=("arbitrary",), vmem_limit_bytes=_vmem_limit(est)),
        name="pool_out",
    )(u, u, qm, km, vm, x, w_pool_bf16, pool_scale.reshape(1, POOL_WIDTH), w_out_bf16)


def _swiglu_step(hn, wg_ref, wu_ref, wd_ref):
    gate = jnp.dot(hn, wg_ref[...].astype(BF16), preferred_element_type=F32)
    up = jnp.dot(hn, wu_ref[...].astype(BF16), preferred_element_type=F32)
    act = (gate * (1.0 / (1.0 + jnp.exp(-gate))) * up).astype(BF16)
    return jnp.dot(act, wd_ref[...].astype(BF16), preferred_element_type=F32)


def _ffn_dense_kernel(x_ref, g_ref, wg_ref, wu_ref, wd_ref, y_ref, hn_ref):
    @pl.when(pl.program_id(1) == 0)
    def _():
        x = x_ref[...]
        hn_ref[...] = (_rms(x) * g_ref[...]).astype(BF16)
        y_ref[...] = x

    y_ref[...] += _swiglu_step(hn_ref[...], wg_ref, wu_ref, wd_ref)


def _ffn_dense(x, gain, w_gu, w_down, tm=FFN_TM, tf=FFN_TF):
    t, d = x.shape
    nf = D_FF // tf
    wbytes = w_gu.dtype.itemsize
    est = 4 * tm * d * 4 + tm * d * 2 + 2 * 3 * d * tf * wbytes + 3 * tm * tf * 4
    return pl.pallas_call(
        _ffn_dense_kernel,
        grid=(t // tm, nf),
        in_specs=[
            pl.BlockSpec((tm, d), lambda i, f: (i, 0)),
            _resident((1, d)),
            pl.BlockSpec((d, tf), lambda i, f: (0, f)),
            pl.BlockSpec((d, tf), lambda i, f: (0, f + nf)),
            pl.BlockSpec((tf, d), lambda i, f: (f, 0)),
        ],
        out_specs=pl.BlockSpec((tm, d), lambda i, f: (i, 0)),
        out_shape=jax.ShapeDtypeStruct((t, d), F32),
        scratch_shapes=[pltpu.VMEM((tm, d), BF16)],
        compiler_params=pltpu.CompilerParams(
            dimension_semantics=("arbitrary", "arbitrary"), vmem_limit_bytes=_vmem_limit(est)),
        name="ffn_dense",
    )(x, gain.reshape(1, d), w_gu, w_gu, w_down)


def _route_kernel(x_ref, g_ref, whi_ref, wlo_ref, tri_ref, hp_ref, info_ref, cnt_ref, carry_ref):
    i = pl.program_id(0)

    @pl.when(i == 0)
    def _():
        carry_ref[...] = jnp.zeros_like(carry_ref)

    hn = _rms(x_ref[...]) * g_ref[...]
    hi = hn.astype(BF16)
    hi32 = hi.astype(F32)
    lo = (hn - hi32).astype(BF16)
    logits = (jnp.dot(hi, whi_ref[...], preferred_element_type=F32)
              + jnp.dot(hi, wlo_ref[...], preferred_element_type=F32)
              + jnp.dot(lo, whi_ref[...], preferred_element_type=F32))
    tm = logits.shape[0]
    lane = lax.broadcasted_iota(I32, (tm, LANES), 1).astype(F32)
    lg = jnp.where(lane < N_EXPERTS, logits, -jnp.inf)
    m1 = jnp.max(lg, axis=-1, keepdims=True)
    i1 = jnp.min(jnp.where(lg == m1, lane, float(LANES)), axis=-1, keepdims=True)
    lg2 = jnp.where(lane == i1, -jnp.inf, lg)
    m2 = jnp.max(lg2, axis=-1, keepdims=True)
    i2 = jnp.min(jnp.where(lg2 == m2, lane, float(LANES)), axis=-1, keepdims=True)
    e = jnp.exp(m2 - m1)
    g1 = 1.0 / (1.0 + e)
    g2 = e / (1.0 + e)
    sel1 = lane == i1
    sel2 = lane == i2
    onehot = jnp.where(jnp.logical_or(sel1, sel2), 1.0, 0.0)
    before = jnp.dot(tri_ref[...], onehot.astype(BF16), preferred_element_type=F32) + carry_ref[0:1, :]
    r1 = jnp.sum(jnp.where(sel1, before, 0.0), axis=-1, keepdims=True)
    r2 = jnp.sum(jnp.where(sel2, before, 0.0), axis=-1, keepdims=True)
    carry_ref[0:1, :] = carry_ref[0:1, :] + jnp.sum(onehot, axis=0, keepdims=True)
    cnt_ref[...] = carry_ref[...]

    info = jnp.where(lane == 0, i1, 0.0)
    info = jnp.where(lane == 1, i2, info)
    info = jnp.where(lane == 2, r1, info)
    info = jnp.where(lane == 3, r2, info)
    info = jnp.where(lane == 4, g1, info)
    info = jnp.where(lane == 5, g2, info)
    info_ref[...] = info

    hp_ref[...] = _pack_halves(hi32)


def _route(x, gain, w_router, tm=ROUTE_TM):
    t, d = x.shape
    wpad = jnp.zeros((d, LANES), F32).at[:, :N_EXPERTS].set(w_router)
    whi = wpad.astype(BF16)
    wlo = (wpad - whi.astype(F32)).astype(BF16)
    tri = (lax.broadcasted_iota(I32, (tm, tm), 0) > lax.broadcasted_iota(I32, (tm, tm), 1)).astype(BF16)
    est = 4 * tm * d * 4 + 2 * tm * d * 2 + 2 * d * LANES * 2 + tm * tm * 2 + 8 * tm * LANES * 4
    return pl.pallas_call(
        _route_kernel,
        grid=(t // tm,),
        in_specs=[
            pl.BlockSpec((tm, d), lambda i: (i, 0)),
            _resident((1, d)),
            _resident((d, LANES)),
            _resident((d, LANES)),
            _resident((tm, tm)),
        ],
        out_specs=[
            pl.BlockSpec((tm, d // 2), lambda i: (i, 0)),
            pl.BlockSpec((tm, LANES), lambda i: (i, 0)),
            pl.BlockSpec((SUBLANES, LANES), lambda i: (0, 0)),
        ],
        out_shape=[
            jax.ShapeDtypeStruct((t, d // 2), U32),
            jax.ShapeDtypeStruct((t, LANES), F32),
            jax.ShapeDtypeStruct((SUBLANES, LANES), F32),
        ],
        scratch_shapes=[pltpu.VMEM((SUBLANES, LANES), F32)],
        compiler_params=pltpu.CompilerParams(
            dimension_semantics=("arbitrary",), vmem_limit_bytes=_vmem_limit(est)),
        name="route",
    )(x, gain.reshape(1, d), whi, wlo, tri)


def _pack_halves(x):
    bits = pltpu.bitcast(x.astype(BF16).astype(F32), U32)
    half = bits.shape[1] // 2
    return (bits[:, :half] & jnp.uint32(0xFFFF0000)) | (bits[:, half:] >> 16)


def _unpack_halves(words):
    return pltpu.bitcast(words & jnp.uint32(0xFFFF0000), F32), pltpu.bitcast(words << 16, F32)


def _for_each_row(tm, fn):
    def body(c, carry):
        for u in range(ROW_UNROLL):
            fn(c * ROW_UNROLL + u)
        return carry

    lax.fori_loop(0, tm // ROW_UNROLL, body, 0)


def _scatter_kernel(pos_ref, hp_ref, xs_hbm_in, xs_hbm, sem):
    del xs_hbm_in
    tm = hp_ref.shape[0]

    def row_copy(r, k):
        return pltpu.make_async_copy(hp_ref.at[pl.ds(r, 1)], xs_hbm.at[pl.ds(pos_ref[0, 2 * r + k], 1)], sem)

    def start(r):
        row_copy(r, 0).start(priority=0)
        row_copy(r, 1).start(priority=1)

    def wait(r):
        row_copy(r, 0).wait()
        row_copy(r, 1).wait()

    _for_each_row(tm, start)
    _for_each_row(tm, wait)


def _scatter_rows(hp, pos, n_slots, tm=ROW_TM):
    t = hp.shape[0]
    pos3 = pos.reshape(t // tm, 1, 2 * tm)
    return pl.pallas_call(
        _scatter_kernel,
        grid=(t // tm,),
        in_specs=[
            pl.BlockSpec((None, 1, 2 * tm), lambda i: (i, 0, 0), memory_space=pltpu.SMEM),
            pl.BlockSpec((tm, hp.shape[1]), lambda i: (i, 0)),
            pl.BlockSpec(memory_space=pl.ANY),
        ],
        out_specs=pl.BlockSpec(memory_space=pl.ANY),
        out_shape=jax.ShapeDtypeStruct((n_slots, hp.shape[1]), U32),
        scratch_shapes=[pltpu.SemaphoreType.DMA(())],
        input_output_aliases={2: 0},
        compiler_params=pltpu.CompilerParams(dimension_semantics=("arbitrary",)),
        name="scatter_rows",
    )(pos3, hp, jnp.zeros((n_slots, hp.shape[1]), U32))


def _combine_kernel(pos_ref, x_ref, info_ref, ys_hbm, y_ref, buf_ref, sem):
    tm, d = x_ref.shape

    def row_copy(r, k):
        return pltpu.make_async_copy(ys_hbm.at[pl.ds(pos_ref[0, 2 * r + k], 1)], buf_ref.at[k, pl.ds(r, 1)], sem)

    def start(r):
        row_copy(r, 0).start(priority=0)
        row_copy(r, 1).start(priority=1)

    def wait(r):
        row_copy(r, 0).wait()
        row_copy(r, 1).wait()

    _for_each_row(tm, start)
    _for_each_row(tm, wait)
    info = info_ref[...]
    g1, g2 = info[:, 4:5], info[:, 5:6]
    a_hi, a_lo = _unpack_halves(buf_ref[0])
    b_hi, b_lo = _unpack_halves(buf_ref[1])
    y_ref[:, :d // 2] = x_ref[:, :d // 2] + g1 * a_hi + g2 * b_hi
    y_ref[:, d // 2:] = x_ref[:, d // 2:] + g1 * a_lo + g2 * b_lo


def _combine(x, info, ys, pos, tm=ROW_TM):
    t, d = x.shape
    pos3 = pos.reshape(t // tm, 1, 2 * tm)
    est = 4 * tm * d * 4 + 2 * tm * LANES * 4 + 2 * tm * d * 2 + 2 * tm * d * 4
    return pl.pallas_call(
        _combine_kernel,
        grid=(t // tm,),
        in_specs=[
            pl.BlockSpec((None, 1, 2 * tm), lambda i: (i, 0, 0), memory_space=pltpu.SMEM),
            pl.BlockSpec((tm, d), lambda i: (i, 0)),
            pl.BlockSpec((tm, LANES), lambda i: (i, 0)),
            pl.BlockSpec(memory_space=pl.ANY),
        ],
        out_specs=pl.BlockSpec((tm, d), lambda i: (i, 0)),
        out_shape=jax.ShapeDtypeStruct((t, d), F32),
        scratch_shapes=[pltpu.VMEM((2, tm, ys.shape[1]), U32), pltpu.SemaphoreType.DMA(())],
        compiler_params=pltpu.CompilerParams(
            dimension_semantics=("arbitrary",), vmem_limit_bytes=_vmem_limit(est)),
        name="combine",
    )(pos3, x, info, ys)


def _ffn_moe_kernel(te_ref, tr_ref, xs_ref, wg_ref, wu_ref, wd_ref, y_ref, hn_ref, acc_ref):
    del te_ref
    i, f = pl.program_id(0), pl.program_id(1)
    rows = tr_ref[i]
    tm, d = acc_ref.shape

    @pl.when(f == 0)
    def _():
        acc_ref[...] = jnp.zeros_like(acc_ref)

    @pl.when(jnp.logical_and(rows > 0, f == 0))
    def _():
        hi, lo = _unpack_halves(xs_ref[...])
        hn_ref[:, :d // 2] = hi.astype(BF16)
        hn_ref[:, d // 2:] = lo.astype(BF16)

    for n_sub in range(1, tm // MOE_SUB + 1):
        live = n_sub * MOE_SUB
        in_range = jnp.logical_and(rows > live - MOE_SUB, rows <= live)

        @pl.when(in_range)
        def _():
            acc_ref[0:live, :] += _swiglu_step(hn_ref[0:live, :], wg_ref, wu_ref, wd_ref)

    @pl.when(f == pl.num_programs(1) - 1)
    def _():
        y_ref[...] = _pack_halves(acc_ref[...])


def _ffn_moe(xs, tile_expert, tile_rows, w_gu, w_down, tm=MOE_TM, tf=MOE_TF):
    n_slots, half = xs.shape
    d = 2 * half
    nf = D_FF // tf
    n_tiles = n_slots // tm
    wbytes = w_gu.dtype.itemsize

    def f_eff(i, f, tr):
        return jnp.where(tr[i] > 0, f, nf - 1)

    est = 4 * tm * half * 4 + tm * d * 2 + tm * d * 4 + 2 * 3 * d * tf * wbytes + 3 * d * tf * 2 + 3 * tm * tf * 4
    return pl.pallas_call(
        _ffn_moe_kernel,
        grid_spec=pltpu.PrefetchScalarGridSpec(
            num_scalar_prefetch=2,
            grid=(n_tiles, nf),
            in_specs=[
                pl.BlockSpec((tm, half), lambda i, f, te, tr: (i, 0)),
                pl.BlockSpec((None, d, tf), lambda i, f, te, tr: (te[i], 0, f_eff(i, f, tr))),
                pl.BlockSpec((None, d, tf), lambda i, f, te, tr: (te[i], 0, f_eff(i, f, tr) + nf)),
                pl.BlockSpec((None, tf, d), lambda i, f, te, tr: (te[i], f_eff(i, f, tr), 0)),
            ],
            out_specs=pl.BlockSpec((tm, half), lambda i, f, te, tr: (i, 0)),
            scratch_shapes=[pltpu.VMEM((tm, d), BF16), pltpu.VMEM((tm, d), F32)],
        ),
        out_shape=jax.ShapeDtypeStruct((n_slots, half), U32),
        compiler_params=pltpu.CompilerParams(
            dimension_semantics=("arbitrary", "arbitrary"), vmem_limit_bytes=_vmem_limit(est)),
        name="ffn_moe",
    )(tile_expert, tile_rows, xs, w_gu, w_gu, w_down)


def _moe_layer(x, gain, w_router, w_gu, w_down, tm=MOE_TM):
    t, _ = x.shape
    hp, info, cnt = _route(x, gain, w_router)
    counts = cnt[0, :N_EXPERTS].astype(I32)
    tiles_per_expert = (counts + tm - 1) // tm
    tile_end = jnp.cumsum(tiles_per_expert)
    group_start = (tile_end - tiles_per_expert) * tm
    n_tiles = (2 * t + N_EXPERTS * (tm - 1)) // tm
    tile_ids = jnp.arange(n_tiles, dtype=I32)
    tile_expert = jnp.minimum(jnp.sum(tile_ids[:, None] >= tile_end[None, :], axis=1), N_EXPERTS - 1).astype(I32)
    last_used_expert = tile_expert[jnp.maximum(tile_end[-1] - 1, 0)]
    used = tile_ids < tile_end[-1]
    tile_expert = jnp.where(used, tile_expert, last_used_expert)
    rows_left = counts[tile_expert] - (tile_ids * tm - group_start[tile_expert])
    tile_rows = jnp.where(used, jnp.clip(rows_left, 0, tm), 0).astype(I32)
    idx = info[:, 0:2].astype(I32)
    rank = info[:, 2:4].astype(I32)
    pos = (group_start[idx] + rank).astype(I32)

    xs = _scatter_rows(hp, pos, n_tiles * tm)
    ys = _ffn_moe(xs, tile_expert, tile_rows, w_gu, w_down)
    return _combine(x, info, ys, pos)


def _mem_kv(mem2d, norm_mem, w_mem_kv, k_gain):
    head_gain = jnp.concatenate([jnp.tile(k_gain, HEADS_PER_GROUP), jnp.ones((GROUP_WIDTH,), F32)])[None, :]
    segments = ((0, GROUP_WIDTH, True, 1), (GROUP_WIDTH, GROUP_WIDTH, False, 1))
    km, vm = _proj(mem2d, norm_mem, w_mem_kv.astype(BF16), head_gain, segments, seq=mem2d.shape[0])
    return km.reshape(-1, N_MEM, GROUP_WIDTH), vm.reshape(-1, N_MEM, GROUP_WIDTH)


def _dilated_mixer(x, mem2d, seq, norm_mix, norm_mem, w_in, qk_norm, w_mem_kv, w_out):
    n_grp = len(DILATIONS)
    head_gain = jnp.concatenate([
        jnp.tile(qk_norm[0], n_grp * HEADS_PER_GROUP), jnp.tile(qk_norm[1], n_grp * HEADS_PER_GROUP),
        jnp.ones((n_grp * GROUP_WIDTH,), F32), jnp.tile(qk_norm[2], HEADS_PER_GROUP)])[None, :]
    segments = tuple((c * GROUP_WIDTH, GROUP_WIDTH, c < 2 * n_grp, DILATIONS[c % n_grp]) for c in range(3 * n_grp))
    segments += ((3 * n_grp * GROUP_WIDTH, GROUP_WIDTH, True, 1),)
    *qkv, qm = _proj(x, norm_mix, w_in.astype(BF16), head_gain, segments, seq)
    km, vm = _mem_kv(mem2d, norm_mem, w_mem_kv, qk_norm[3])
    batch = x.shape[0] // seq
    attn = []
    for g, dilation in enumerate(DILATIONS):
        q, k, v = (a.reshape(batch, dilation, seq // dilation, GROUP_WIDTH) for a in qkv[g::n_grp])
        attn.append(_dil_attn(q, k, v))
    return _dil_out(x, attn, qm, km, vm, w_out.astype(BF16), seq)


def _pooling_mixer(x, mem2d, seq, norm_mix, norm_mem, w_in, w_pool, pool_scale, qk_norm, w_mem_kv, w_out):
    head_gain = jnp.concatenate([jnp.ones((POOL_WIDTH,), F32), jnp.tile(qk_norm[0], HEADS_PER_GROUP)])[None, :]
    segments = ((0, POOL_WIDTH, False, 1), (POOL_WIDTH, GROUP_WIDTH, True, 1))
    u, qm = _proj(x, norm_mix, w_in.astype(BF16), head_gain, segments, seq)
    km, vm = _mem_kv(mem2d, norm_mem, w_mem_kv, qk_norm[1])
    return _pool_out(x, u, qm, km, vm, w_pool.astype(BF16), pool_scale, w_out.astype(BF16), seq)


def kernel(x, mem, l0_norm_mix, l0_norm_mem, l0_w_in, l0_qk_norm, l0_w_mem_kv, l0_w_out, l0_norm_ffn, l0_w_gu, l0_w_down, l1_norm_mix, l1_norm_mem, l1_w_in, l1_w_pool, l1_pool_scale, l1_qk_norm, l1_w_mem_kv, l1_w_out, l1_norm_ffn, l1_w_router, l1_w_gu_e, l1_w_down_e, l2_norm_mix, l2_norm_mem, l2_w_in, l2_qk_norm, l2_w_mem_kv, l2_w_out, l2_norm_ffn, l2_w_gu, l2_w_down, l3_norm_mix, l3_norm_mem, l3_w_in, l3_w_pool, l3_pool_scale, l3_qk_norm, l3_w_mem_kv, l3_w_out, l3_norm_ffn, l3_w_router, l3_w_gu_e, l3_w_down_e):
    batch, seq, d = x.shape
    h = x.reshape(batch * seq, d)
    mem2d = mem.reshape(-1, d)

    h = _dilated_mixer(h, mem2d, seq, l0_norm_mix, l0_norm_mem, l0_w_in, l0_qk_norm, l0_w_mem_kv, l0_w_out)
    h = _ffn_dense(h, l0_norm_ffn, l0_w_gu.astype(BF16), l0_w_down.astype(BF16))
    h = _pooling_mixer(h, mem2d, seq, l1_norm_mix, l1_norm_mem, l1_w_in, l1_w_pool, l1_pool_scale,
                       l1_qk_norm, l1_w_mem_kv, l1_w_out)
    h = _moe_layer(h, l1_norm_ffn, l1_w_router, l1_w_gu_e, l1_w_down_e)
    h = _dilated_mixer(h, mem2d, seq, l2_norm_mix, l2_norm_mem, l2_w_in, l2_qk_norm, l2_w_mem_kv, l2_w_out)
    h = _ffn_dense(h, l2_norm_ffn, l2_w_gu.astype(BF16), l2_w_down.astype(BF16))
    h = _pooling_mixer(h, mem2d, seq, l3_norm_mix, l3_norm_mem, l3_w_in, l3_w_pool, l3_pool_scale,
                       l3_qk_norm, l3_w_mem_kv, l3_w_out)
    h = _moe_layer(h, l3_norm_ffn, l3_w_router, l3_w_gu_e, l3_w_down_e)
    return h.reshape(batch, seq, d)
```

```python
import functools

import jax
import jax.numpy as jnp
from jax import lax
from jax.experimental import pallas as pl
from jax.experimental.pallas import tpu as pltpu

F32 = jnp.float32
BF16 = jnp.bfloat16
U32 = jnp.uint32
I32 = jnp.int32

HEAD_DIM = 128
HEADS_PER_GROUP = 4
GROUP_WIDTH = HEADS_PER_GROUP * HEAD_DIM
DILATIONS = (1, 4, 16)
ATTN_BLOCK = 128
N_MEM = 256
POOL_WINDOWS = (2, 4, 8, 16)
POOL_GROUP = 384
POOL_WIDTH = 4 * POOL_GROUP
POOL_HALO = 16
D_FF = 7168
N_EXPERTS = 8
EPS = 1e-6
SCORE_SCALE = HEAD_DIM ** -0.5

LANES = 128
SUBLANES = 8
V7X_VMEM_BYTES = 64 * 1024 * 1024

PROJ_TM = 512
ATTN_QB = 512
OUT_TM = 512
FFN_TM, FFN_TF = 1024, 512
MOE_TM, MOE_SUB = 1024, 512
MOE_TF = 512
ROUTE_TM = 512
ROW_TM = 1024
ROW_UNROLL = 8


def _vmem_limit(nbytes):
    return int(min(nbytes * 5 // 4 + (6 << 20), V7X_VMEM_BYTES - (4 << 20)))


def _rms(x, eps=EPS):
    return x * lax.rsqrt(jnp.mean(x * x, axis=-1, keepdims=True) + eps)


def _resident(shape):
    return pl.BlockSpec(shape, lambda *_: (0,) * len(shape), pipeline_mode=pl.Buffered(1))


def _proj_kernel(x_ref, g_ref, w_ref, hg_ref, *refs, segments):
    outs, (hn_ref, stage_ref) = refs[:len(segments)], refs[len(segments):]
    tm = x_ref.shape[0]
    hn_ref[...] = (_rms(x_ref[...]) * g_ref[...]).astype(BF16)
    for (col0, width, norm, dilation), o_ref in zip(segments, outs):
        for sub in range(width // GROUP_WIDTH):
            c0 = col0 + sub * GROUP_WIDTH
            acc = jnp.dot(hn_ref[...], w_ref[:, c0:c0 + GROUP_WIDTH], preferred_element_type=F32)
            for h in range(HEADS_PER_GROUP):
                cols = slice(h * HEAD_DIM, (h + 1) * HEAD_DIM)
                val = acc[:, cols]
                if norm:
                    val = _rms(val) * hg_ref[:, c0 + h * HEAD_DIM:c0 + (h + 1) * HEAD_DIM]
                if dilation == 1:
                    o_ref[:, sub * GROUP_WIDTH + h * HEAD_DIM:sub * GROUP_WIDTH + (h + 1) * HEAD_DIM] = val.astype(BF16)
                else:
                    stage_ref[h] = val
                    for r in range(dilation):
                        o_ref[r, :, cols] = stage_ref[h, pl.ds(r, tm // dilation, stride=dilation), :].astype(BF16)


def _proj(x, gain, w_bf16, head_gain, segments, seq, tm=PROJ_TM):
    t, d = x.shape
    tm = min(tm, t)
    tiles_per_seq = max(seq // tm, 1)
    out_specs, out_shapes = [], []
    for _, width, _, dilation in segments:
        if dilation == 1:
            out_specs.append(pl.BlockSpec((tm, width), lambda i: (i, 0)))
            out_shapes.append(jax.ShapeDtypeStruct((t, width), BF16))
        else:
            out_specs.append(pl.BlockSpec((None, dilation, tm // dilation, width),
                                          lambda i: (i // tiles_per_seq, 0, i % tiles_per_seq, 0)))
            out_shapes.append(jax.ShapeDtypeStruct((t // seq, dilation, seq // dilation, width), BF16))
    n_out_cols = sum(s[1] for s in segments)
    est = (2 * tm * d * 4 + tm * d * 2 + w_bf16.size * 2 + 4 * tm * n_out_cols + tm * GROUP_WIDTH * 4
           + 4 * tm * GROUP_WIDTH * 4)
    return pl.pallas_call(
        functools.partial(_proj_kernel, segments=tuple(segments)),
        grid=(t // tm,),
        in_specs=[
            pl.BlockSpec((tm, d), lambda i: (i, 0)),
            _resident((1, d)),
            _resident(w_bf16.shape),
            _resident(head_gain.shape),
        ],
        out_specs=out_specs,
        out_shape=out_shapes,
        scratch_shapes=[pltpu.VMEM((tm, d), BF16), pltpu.VMEM((HEADS_PER_GROUP, tm, HEAD_DIM), F32)],
        compiler_params=pltpu.CompilerParams(
            dimension_semantics=("arbitrary",), vmem_limit_bytes=_vmem_limit(est)),
        name="proj",
    )(x, gain.reshape(1, d), w_bf16, head_gain)


def _dil_attn_kernel(q_ref, kp_ref, k_ref, vp_ref, v_ref, o_ref, l_ref, *, qb):
    has_prev = pl.program_id(2) > 0
    qi = lax.broadcasted_iota(I32, (ATTN_BLOCK, 2 * ATTN_BLOCK), 0)
    kj = lax.broadcasted_iota(I32, (ATTN_BLOCK, 2 * ATTN_BLOCK), 1)
    in_prev = kj < ATTN_BLOCK
    kk = jnp.where(in_prev, kj, kj - ATTN_BLOCK)
    band = jnp.logical_and(kk >= jnp.where(in_prev, qi, 0), kk <= jnp.where(in_prev, ATTN_BLOCK, qi))
    band_first = jnp.logical_and(band, jnp.logical_or(has_prev, jnp.logical_not(in_prev)))
    for jb in range(qb // ATTN_BLOCK):
        rows = slice(jb * ATTN_BLOCK, (jb + 1) * ATTN_BLOCK)
        for h in range(HEADS_PER_GROUP):
            cols = slice(h * HEAD_DIM, (h + 1) * HEAD_DIM)
            q = q_ref[rows, cols]
            if jb == 0:
                keys = jnp.concatenate([kp_ref[:, cols], k_ref[rows, cols]], axis=0)
                vals = jnp.concatenate([vp_ref[:, cols], v_ref[rows, cols]], axis=0)
                mask = band_first
            else:
                both = slice((jb - 1) * ATTN_BLOCK, (jb + 1) * ATTN_BLOCK)
                keys = k_ref[both, cols]
                vals = v_ref[both, cols]
                mask = band
            s = lax.dot_general(q, keys, (((1,), (1,)), ((), ())), preferred_element_type=F32)
            s = jnp.where(mask, s * SCORE_SCALE, -jnp.inf)
            m = jnp.max(s, axis=-1, keepdims=True)
            p = jnp.exp(s - m)
            den = jnp.sum(p, axis=-1, keepdims=True)
            o = jnp.dot(p.astype(BF16), vals, preferred_element_type=F32)
            o_ref[rows, cols] = o / den
            l_ref[rows, cols] = jnp.broadcast_to(m + jnp.log(den), (ATTN_BLOCK, HEAD_DIM))


def _dil_attn(q, k, v, qb=ATTN_QB):
    batch, dilation, length, width = q.shape
    qb = min(qb, length)
    bpq = qb // ATTN_BLOCK
    cur = pl.BlockSpec((None, None, qb, width), lambda b, r, i: (b, r, i, 0))
    prev = pl.BlockSpec((None, None, ATTN_BLOCK, width), lambda b, r, i: (b, r, jnp.maximum(i * bpq - 1, 0), 0))
    out_sds = jax.ShapeDtypeStruct(q.shape, F32)
    est = 2 * (3 * qb + 2 * ATTN_BLOCK) * width * 2 + 2 * 2 * qb * width * 4
    return pl.pallas_call(
        functools.partial(_dil_attn_kernel, qb=qb),
        grid=(batch, dilation, length // qb),
        in_specs=[cur, prev, cur, prev, cur],
        out_specs=[cur, cur],
        out_shape=[out_sds, out_sds],
        compiler_params=pltpu.CompilerParams(
            dimension_semantics=("arbitrary", "arbitrary", "arbitrary"),
            vmem_limit_bytes=_vmem_limit(est)),
        name="dil_attn",
    )(q, k, k, v, v)


def _mem_attention(qm_ref, km_ref, vm_ref, cat_ref, col0):
    for h in range(HEADS_PER_GROUP):
        cols = slice(h * HEAD_DIM, (h + 1) * HEAD_DIM)
        s = lax.dot_general(qm_ref[:, cols], km_ref[:, cols], (((1,), (1,)), ((), ())),
                            preferred_element_type=F32) * SCORE_SCALE
        m = jnp.max(s, axis=-1, keepdims=True)
        p = jnp.exp(s - m)
        den = jnp.sum(p, axis=-1, keepdims=True)
        o = jnp.dot(p.astype(BF16), vm_ref[:, cols], preferred_element_type=F32)
        cat_ref[:, col0 + h * HEAD_DIM:col0 + (h + 1) * HEAD_DIM] = (o / den).astype(BF16)


def _to_token_order(src_ref, dst_ref):
    dilation, per_res, _ = src_ref.shape
    for h in range(HEADS_PER_GROUP):
        for r in range(dilation):
            dst_ref[h, pl.ds(r, per_res, stride=dilation), :] = src_ref[r, :, h * HEAD_DIM:(h + 1) * HEAD_DIM]


def _dil_out_kernel(o0_ref, l0_ref, o1_ref, l1_ref, o2_ref, l2_ref, qm_ref, km_ref, vm_ref,
                    x_ref, w_ref, y_ref, o1s, l1s, o2s, l2s, cat_ref):
    for src, dst in ((o1_ref, o1s), (l1_ref, l1s), (o2_ref, o2s), (l2_ref, l2s)):
        _to_token_order(src, dst)
    for h in range(HEADS_PER_GROUP):
        cols = slice(h * HEAD_DIM, (h + 1) * HEAD_DIM)
        l0, l1, l2 = l0_ref[:, cols], l1s[h], l2s[h]
        mx = jnp.maximum(jnp.maximum(l0, l1), l2)
        e0, e1, e2 = jnp.exp(l0 - mx), jnp.exp(l1 - mx), jnp.exp(l2 - mx)
        den = e0 + e1 + e2
        dil = (e0 / den) * o0_ref[:, cols] + (e1 / den) * o1s[h] + (e2 / den) * o2s[h]
        cat_ref[:, cols] = dil.astype(BF16)
    _mem_attention(qm_ref, km_ref, vm_ref, cat_ref, GROUP_WIDTH)
    y_ref[...] = x_ref[...] + jnp.dot(cat_ref[...], w_ref[...], preferred_element_type=F32)


def _dil_out(x, attn, qm, km, vm, w_out_bf16, seq, tm=OUT_TM):
    t, d = x.shape
    tiles_per_seq = seq // tm

    def grp(dilation):
        return pl.BlockSpec((None, dilation, tm // dilation, GROUP_WIDTH),
                            lambda i: (i // tiles_per_seq, 0, i % tiles_per_seq, 0))

    tok = pl.BlockSpec((tm, GROUP_WIDTH), lambda i: (i, 0))
    mem = pl.BlockSpec((None, N_MEM, GROUP_WIDTH), lambda i: (i // tiles_per_seq, 0, 0))
    (o0, l0), (o1, l1), (o2, l2) = attn
    stage = pltpu.VMEM((HEADS_PER_GROUP, tm, HEAD_DIM), F32)
    est = (2 * 6 * tm * GROUP_WIDTH * 4 + 4 * tm * GROUP_WIDTH * 4 + 2 * tm * GROUP_WIDTH * 2
           + 4 * N_MEM * GROUP_WIDTH * 2 + 4 * tm * d * 4 + w_out_bf16.size * 2 + 3 * tm * d * 4)
    return pl.pallas_call(
        _dil_out_kernel,
        grid=(t // tm,),
        in_specs=[tok, tok, grp(DILATIONS[1]), grp(DILATIONS[1]), grp(DILATIONS[2]), grp(DILATIONS[2]),
                  tok, mem, mem, pl.BlockSpec((tm, d), lambda i: (i, 0)), _resident(w_out_bf16.shape)],
        out_specs=pl.BlockSpec((tm, d), lambda i: (i, 0)),
        out_shape=jax.ShapeDtypeStruct((t, d), F32),
        scratch_shapes=[stage, stage, stage, stage, pltpu.VMEM((tm, w_out_bf16.shape[0]), BF16)],
        compiler_params=pltpu.CompilerParams(
            dimension_semantics=("arbitrary",), vmem_limit_bytes=_vmem_limit(est)),
        name="dil_out",
    )(o0.reshape(t, GROUP_WIDTH), l0.reshape(t, GROUP_WIDTH), o1, l1, o2, l2, qm, km, vm, x, w_out_bf16)


def _pool_out_kernel(u_ref, halo_ref, qm_ref, km_ref, vm_ref, x_ref, wp_ref, ps_ref, w_ref, y_ref, cat_ref,
                     *, tiles_per_seq):
    tm = u_ref.shape[0]
    tile_in_seq = pl.program_id(0) % tiles_per_seq
    halo = jnp.where(tile_in_seq > 0, halo_ref[...].astype(F32), 0.0)
    full = jnp.concatenate([halo, u_ref[...].astype(F32)], axis=0)
    pos = tile_in_seq * tm + lax.broadcasted_iota(I32, (tm, 1), 0)
    acc = full
    for g, window in enumerate(POOL_WINDOWS):
        lo = g * POOL_GROUP
        acc = acc[:, (POOL_GROUP if g else 0):]
        acc = acc + pltpu.roll(acc, window // 2, 0)
        count = jnp.minimum(pos + 1, window).astype(F32)
        pooled = acc[POOL_HALO:, :POOL_GROUP] / count - full[POOL_HALO:, lo:lo + POOL_GROUP]
        po = jnp.dot(pooled.astype(BF16), wp_ref[g], preferred_element_type=F32)
        cat_ref[:, lo:lo + POOL_GROUP] = (po * ps_ref[:, lo:lo + POOL_GROUP]).astype(BF16)
    _mem_attention(qm_ref, km_ref, vm_ref, cat_ref, POOL_WIDTH)
    y_ref[...] = x_ref[...] + jnp.dot(cat_ref[...], w_ref[...], preferred_element_type=F32)


def _pool_out(x, u, qm, km, vm, w_pool_bf16, pool_scale, w_out_bf16, seq, tm=OUT_TM):
    t, d = x.shape
    tiles_per_seq = seq // tm
    halo_blocks = tm // POOL_HALO
    mem = pl.BlockSpec((None, N_MEM, GROUP_WIDTH), lambda i: (i // tiles_per_seq, 0, 0))
    est = (2 * tm * d * 2 + 4 * N_MEM * GROUP_WIDTH * 2 + 4 * tm * d * 4 + w_out_bf16.size * 2
           + w_pool_bf16.size * 2 + 8 * tm * POOL_WIDTH * 4)
    return pl.pallas_call(
        functools.partial(_pool_out_kernel, tiles_per_seq=tiles_per_seq),
        grid=(t // tm,),
        in_specs=[
            pl.BlockSpec((tm, POOL_WIDTH), lambda i: (i, 0)),
            pl.BlockSpec((POOL_HALO, POOL_WIDTH), lambda i: (jnp.maximum(i * halo_blocks - 1, 0), 0)),
            pl.BlockSpec((tm, GROUP_WIDTH), lambda i: (i, 0)),
            mem, mem,
            pl.BlockSpec((tm, d), lambda i: (i, 0)),
            _resident(w_pool_bf16.shape),
            _resident((1, POOL_WIDTH)),
            _resident(w_out_bf16.shape),
        ],
        out_specs=pl.BlockSpec((tm, d), lambda i: (i, 0)),
        out_shape=jax.ShapeDtypeStruct((t, d), F32),
        scratch_shapes=[pltpu.VMEM((tm, w_out_bf16.shape[0]), BF16)],
        compiler_params=pltpu.CompilerParams(
            dimension_semantics=("arbitrary",), vmem_limit_bytes=_vmem_limit(est)),
        name="pool_out",
    )(u, u, qm, km, vm, x, w_pool_bf16, pool_scale.reshape(1, POOL_WIDTH), w_out_bf16)


def _swiglu_step(hn, wg_ref, wu_ref, wd_ref):
    gate = jnp.dot(hn, wg_ref[...].astype(BF16), preferred_element_type=F32)
    up = jnp.dot(hn, wu_ref[...].astype(BF16), preferred_element_type=F32)
    act = (gate * (1.0 / (1.0 + jnp.exp(-gate))) * up).astype(BF16)
    return jnp.dot(act, wd_ref[...].astype(BF16), preferred_element_type=F32)


def _ffn_dense_kernel(x_ref, g_ref, wg_ref, wu_ref, wd_ref, y_ref, hn_ref):
    @pl.when(pl.program_id(1) == 0)
    def _():
        x = x_ref[...]
        hn_ref[...] = (_rms(x) * g_ref[...]).astype(BF16)
        y_ref[...] = x

    y_ref[...] += _swiglu_step(hn_ref[...], wg_ref, wu_ref, wd_ref)


def _ffn_dense(x, gain, w_gu, w_down, tm=FFN_TM, tf=FFN_TF):
    t, d = x.shape
    nf = D_FF // tf
    wbytes = w_gu.dtype.itemsize
    est = 4 * tm * d * 4 + tm * d * 2 + 2 * 3 * d * tf * wbytes + 3 * tm * tf * 4
    return pl.pallas_call(
        _ffn_dense_kernel,
        grid=(t // tm, nf),
        in_specs=[
            pl.BlockSpec((tm, d), lambda i, f: (i, 0)),
            _resident((1, d)),
            pl.BlockSpec((d, tf), lambda i, f: (0, f)),
            pl.BlockSpec((d, tf), lambda i, f: (0, f + nf)),
            pl.BlockSpec((tf, d), lambda i, f: (f, 0)),
        ],
        out_specs=pl.BlockSpec((tm, d), lambda i, f: (i, 0)),
        out_shape=jax.ShapeDtypeStruct((t, d), F32),
        scratch_shapes=[pltpu.VMEM((tm, d), BF16)],
        compiler_params=pltpu.CompilerParams(
            dimension_semantics=("arbitrary", "arbitrary"), vmem_limit_bytes=_vmem_limit(est)),
        name="ffn_dense",
    )(x, gain.reshape(1, d), w_gu, w_gu, w_down)


def _route_kernel(x_ref, g_ref, whi_ref, wlo_ref, tri_ref, hp_ref, info_ref, cnt_ref, carry_ref):
    i = pl.program_id(0)

    @pl.when(i == 0)
    def _():
        carry_ref[...] = jnp.zeros_like(carry_ref)

    hn = _rms(x_ref[...]) * g_ref[...]
    hi = hn.astype(BF16)
    hi32 = hi.astype(F32)
    lo = (hn - hi32).astype(BF16)
    logits = (jnp.dot(hi, whi_ref[...], preferred_element_type=F32)
              + jnp.dot(hi, wlo_ref[...], preferred_element_type=F32)
              + jnp.dot(lo, whi_ref[...], preferred_element_type=F32))
    tm = logits.shape[0]
    lane = lax.broadcasted_iota(I32, (tm, LANES), 1).astype(F32)
    lg = jnp.where(lane < N_EXPERTS, logits, -jnp.inf)
    m1 = jnp.max(lg, axis=-1, keepdims=True)
    i1 = jnp.min(jnp.where(lg == m1, lane, float(LANES)), axis=-1, keepdims=True)
    lg2 = jnp.where(lane == i1, -jnp.inf, lg)
    m2 = jnp.max(lg2, axis=-1, keepdims=True)
    i2 = jnp.min(jnp.where(lg2 == m2, lane, float(LANES)), axis=-1, keepdims=True)
    e = jnp.exp(m2 - m1)
    g1 = 1.0 / (1.0 + e)
    g2 = e / (1.0 + e)
    sel1 = lane == i1
    sel2 = lane == i2
    onehot = jnp.where(jnp.logical_or(sel1, sel2), 1.0, 0.0)
    before = jnp.dot(tri_ref[...], onehot.astype(BF16), preferred_element_type=F32) + carry_ref[0:1, :]
    r1 = jnp.sum(jnp.where(sel1, before, 0.0), axis=-1, keepdims=True)
    r2 = jnp.sum(jnp.where(sel2, before, 0.0), axis=-1, keepdims=True)
    carry_ref[0:1, :] = carry_ref[0:1, :] + jnp.sum(onehot, axis=0, keepdims=True)
    cnt_ref[...] = carry_ref[...]

    info = jnp.where(lane == 0, i1, 0.0)
    info = jnp.where(lane == 1, i2, info)
    info = jnp.where(lane == 2, r1, info)
    info = jnp.where(lane == 3, r2, info)
    info = jnp.where(lane == 4, g1, info)
    info = jnp.where(lane == 5, g2, info)
    info_ref[...] = info

    hp_ref[...] = _pack_halves(hi32)


def _route(x, gain, w_router, tm=ROUTE_TM):
    t, d = x.shape
    wpad = jnp.zeros((d, LANES), F32).at[:, :N_EXPERTS].set(w_router)
    whi = wpad.astype(BF16)
    wlo = (wpad - whi.astype(F32)).astype(BF16)
    tri = (lax.broadcasted_iota(I32, (tm, tm), 0) > lax.broadcasted_iota(I32, (tm, tm), 1)).astype(BF16)
    est = 4 * tm * d * 4 + 2 * tm * d * 2 + 2 * d * LANES * 2 + tm * tm * 2 + 8 * tm * LANES * 4
    return pl.pallas_call(
        _route_kernel,
        grid=(t // tm,),
        in_specs=[
            pl.BlockSpec((tm, d), lambda i: (i, 0)),
            _resident((1, d)),
            _resident((d, LANES)),
            _resident((d, LANES)),
            _resident((tm, tm)),
        ],
        out_specs=[
            pl.BlockSpec((tm, d // 2), lambda i: (i, 0)),
            pl.BlockSpec((tm, LANES), lambda i: (i, 0)),
            pl.BlockSpec((SUBLANES, LANES), lambda i: (0, 0)),
        ],
        out_shape=[
            jax.ShapeDtypeStruct((t, d // 2), U32),
            jax.ShapeDtypeStruct((t, LANES), F32),
            jax.ShapeDtypeStruct((SUBLANES, LANES), F32),
        ],
        scratch_shapes=[pltpu.VMEM((SUBLANES, LANES), F32)],
        compiler_params=pltpu.CompilerParams(
            dimension_semantics=("arbitrary",), vmem_limit_bytes=_vmem_limit(est)),
        name="route",
    )(x, gain.reshape(1, d), whi, wlo, tri)


def _pack_halves(x):
    bits = pltpu.bitcast(x.astype(BF16).astype(F32), U32)
    half = bits.shape[1] // 2
    return (bits[:, :half] & jnp.uint32(0xFFFF0000)) | (bits[:, half:] >> 16)


def _unpack_halves(words):
    return pltpu.bitcast(words & jnp.uint32(0xFFFF0000), F32), pltpu.bitcast(words << 16, F32)


def _for_each_row(tm, fn):
    def body(c, carry):
        for u in range(ROW_UNROLL):
            fn(c * ROW_UNROLL + u)
        return carry

    lax.fori_loop(0, tm // ROW_UNROLL, body, 0)


def _scatter_kernel(pos_ref, hp_ref, xs_hbm_in, xs_hbm, sem):
    del xs_hbm_in
    tm = hp_ref.shape[0]

    def row_copy(r, k):
        return pltpu.make_async_copy(hp_ref.at[pl.ds(r, 1)], xs_hbm.at[pl.ds(pos_ref[0, 2 * r + k], 1)], sem)

    def start(r):
        row_copy(r, 0).start(priority=0)
        row_copy(r, 1).start(priority=1)

    def wait(r):
        row_copy(r, 0).wait()
        row_copy(r, 1).wait()

    _for_each_row(tm, start)
    _for_each_row(tm, wait)


def _scatter_rows(hp, pos, n_slots, tm=ROW_TM):
    t = hp.shape[0]
    pos3 = pos.reshape(t // tm, 1, 2 * tm)
    return pl.pallas_call(
        _scatter_kernel,
        grid=(t // tm,),
        in_specs=[
            pl.BlockSpec((None, 1, 2 * tm), lambda i: (i, 0, 0), memory_space=pltpu.SMEM),
            pl.BlockSpec((tm, hp.shape[1]), lambda i: (i, 0)),
            pl.BlockSpec(memory_space=pl.ANY),
        ],
        out_specs=pl.BlockSpec(memory_space=pl.ANY),
        out_shape=jax.ShapeDtypeStruct((n_slots, hp.shape[1]), U32),
        scratch_shapes=[pltpu.SemaphoreType.DMA(())],
        input_output_aliases={2: 0},
        compiler_params=pltpu.CompilerParams(dimension_semantics=("arbitrary",)),
        name="scatter_rows",
    )(pos3, hp, jnp.zeros((n_slots, hp.shape[1]), U32))


def _combine_kernel(pos_ref, x_ref, info_ref, ys_hbm, y_ref, buf_ref, sem):
    tm, d = x_ref.shape

    def row_copy(r, k):
        return pltpu.make_async_copy(ys_hbm.at[pl.ds(pos_ref[0, 2 * r + k], 1)], buf_ref.at[k, pl.ds(r, 1)], sem)

    def start(r):
        row_copy(r, 0).start(priority=0)
        row_copy(r, 1).start(priority=1)

    def wait(r):
        row_copy(r, 0).wait()
        row_copy(r, 1).wait()

    _for_each_row(tm, start)
    _for_each_row(tm, wait)
    info = info_ref[...]
    g1, g2 = info[:, 4:5], info[:, 5:6]
    a_hi, a_lo = _unpack_halves(buf_ref[0])
    b_hi, b_lo = _unpack_halves(buf_ref[1])
    y_ref[:, :d // 2] = x_ref[:, :d // 2] + g1 * a_hi + g2 * b_hi
    y_ref[:, d // 2:] = x_ref[:, d // 2:] + g1 * a_lo + g2 * b_lo


def _combine(x, info, ys, pos, tm=ROW_TM):
    t, d = x.shape
    pos3 = pos.reshape(t // tm, 1, 2 * tm)
    est = 4 * tm * d * 4 + 2 * tm * LANES * 4 + 2 * tm * d * 2 + 2 * tm * d * 4
    return pl.pallas_call(
        _combine_kernel,
        grid=(t // tm,),
        in_specs=[
            pl.BlockSpec((None, 1, 2 * tm), lambda i: (i, 0, 0), memory_space=pltpu.SMEM),
            pl.BlockSpec((tm, d), lambda i: (i, 0)),
            pl.BlockSpec((tm, LANES), lambda i: (i, 0)),
            pl.BlockSpec(memory_space=pl.ANY),
        ],
        out_specs=pl.BlockSpec((tm, d), lambda i: (i, 0)),
        out_shape=jax.ShapeDtypeStruct((t, d), F32),
        scratch_shapes=[pltpu.VMEM((2, tm, ys.shape[1]), U32), pltpu.SemaphoreType.DMA(())],
        compiler_params=pltpu.CompilerParams(
            dimension_semantics=("arbitrary",), vmem_limit_bytes=_vmem_limit(est)),
        name="combine",
    )(pos3, x, info, ys)


def _ffn_moe_kernel(te_ref, tr_ref, xs_ref, wg_ref, wu_ref, wd_ref, y_ref, hn_ref, acc_ref):
    del te_ref
    i, f = pl.program_id(0), pl.program_id(1)
    rows = tr_ref[i]
    tm, d = acc_ref.shape

    @pl.when(f == 0)
    def _():
        acc_ref[...] = jnp.zeros_like(acc_ref)

    @pl.when(jnp.logical_and(rows > 0, f == 0))
    def _():
        hi, lo = _unpack_halves(xs_ref[...])
        hn_ref[:, :d // 2] = hi.astype(BF16)
        hn_ref[:, d // 2:] = lo.astype(BF16)

    for n_sub in range(1, tm // MOE_SUB + 1):
        live = n_sub * MOE_SUB
        in_range = jnp.logical_and(rows > live - MOE_SUB, rows <= live)

        @pl.when(in_range)
        def _():
            acc_ref[0:live, :] += _swiglu_step(hn_ref[0:live, :], wg_ref, wu_ref, wd_ref)

    @pl.when(f == pl.num_programs(1) - 1)
    def _():
        y_ref[...] = _pack_halves(acc_ref[...])


def _ffn_moe(xs, tile_expert, tile_rows, w_gu, w_down, tm=MOE_TM, tf=MOE_TF):
    n_slots, half = xs.shape
    d = 2 * half
    nf = D_FF // tf
    n_tiles = n_slots // tm
    wbytes = w_gu.dtype.itemsize

    def f_eff(i, f, tr):
        return jnp.where(tr[i] > 0, f, nf - 1)

    est = 4 * tm * half * 4 + tm * d * 2 + tm * d * 4 + 2 * 3 * d * tf * wbytes + 3 * d * tf * 2 + 3 * tm * tf * 4
    return pl.pallas_call(
        _ffn_moe_kernel,
        grid_spec=pltpu.PrefetchScalarGridSpec(
            num_scalar_prefetch=2,
            grid=(n_tiles, nf),
            in_specs=[
                pl.BlockSpec((tm, half), lambda i, f, te, tr: (i, 0)),
                pl.BlockSpec((None, d, tf), lambda i, f, te, tr: (te[i], 0, f_eff(i, f, tr))),
                pl.BlockSpec((None, d, tf), lambda i, f, te, tr: (te[i], 0, f_eff(i, f, tr) + nf)),
                pl.BlockSpec((None, tf, d), lambda i, f, te, tr: (te[i], f_eff(i, f, tr), 0)),
            ],
            out_specs=pl.BlockSpec((tm, half), lambda i, f, te, tr: (i, 0)),
            scratch_shapes=[pltpu.VMEM((tm, d), BF16), pltpu.VMEM((tm, d), F32)],
        ),
        out_shape=jax.ShapeDtypeStruct((n_slots, half), U32),
        compiler_params=pltpu.CompilerParams(
            dimension_semantics=("arbitrary", "arbitrary"), vmem_limit_bytes=_vmem_limit(est)),
        name="ffn_moe",
    )(tile_expert, tile_rows, xs, w_gu, w_gu, w_down)


def _moe_layer(x, gain, w_router, w_gu, w_down, tm=MOE_TM):
    t, _ = x.shape
    hp, info, cnt = _route(x, gain, w_router)
    counts = cnt[0, :N_EXPERTS].astype(I32)
    tiles_per_expert = (counts + tm - 1) // tm
    tile_end = jnp.cumsum(tiles_per_expert)
    group_start = (tile_end - tiles_per_expert) * tm
    n_tiles = (2 * t + N_EXPERTS * (tm - 1)) // tm
    tile_ids = jnp.arange(n_tiles, dtype=I32)
    tile_expert = jnp.minimum(jnp.sum(tile_ids[:, None] >= tile_end[None, :], axis=1), N_EXPERTS - 1).astype(I32)
    last_used_expert = tile_expert[jnp.maximum(tile_end[-1] - 1, 0)]
    used = tile_ids < tile_end[-1]
    tile_expert = jnp.where(used, tile_expert, last_used_expert)
    rows_left = counts[tile_expert] - (tile_ids * tm - group_start[tile_expert])
    tile_rows = jnp.where(used, jnp.clip(rows_left, 0, tm), 0).astype(I32)
    idx = info[:, 0:2].astype(I32)
    rank = info[:, 2:4].astype(I32)
    pos = (group_start[idx] + rank).astype(I32)

    xs = _scatter_rows(hp, pos, n_tiles * tm)
    ys = _ffn_moe(xs, tile_expert, tile_rows, w_gu, w_down)
    return _combine(x, info, ys, pos)


def _mem_kv(mem2d, norm_mem, w_mem_kv, k_gain):
    head_gain = jnp.concatenate([jnp.tile(k_gain, HEADS_PER_GROUP), jnp.ones((GROUP_WIDTH,), F32)])[None, :]
    segments = ((0, GROUP_WIDTH, True, 1), (GROUP_WIDTH, GROUP_WIDTH, False, 1))
    km, vm = _proj(mem2d, norm_mem, w_mem_kv.astype(BF16), head_gain, segments, seq=mem2d.shape[0])
    return km.reshape(-1, N_MEM, GROUP_WIDTH), vm.reshape(-1, N_MEM, GROUP_WIDTH)


def _dilated_mixer(x, mem2d, seq, norm_mix, norm_mem, w_in, qk_norm, w_mem_kv, w_out):
    n_grp = len(DILATIONS)
    head_gain = jnp.concatenate([
        jnp.tile(qk_norm[0], n_grp * HEADS_PER_GROUP), jnp.tile(qk_norm[1], n_grp * HEADS_PER_GROUP),
        jnp.ones((n_grp * GROUP_WIDTH,), F32), jnp.tile(qk_norm[2], HEADS_PER_GROUP)])[None, :]
    segments = tuple((c * GROUP_WIDTH, GROUP_WIDTH, c < 2 * n_grp, DILATIONS[c % n_grp]) for c in range(3 * n_grp))
    segments += ((3 * n_grp * GROUP_WIDTH, GROUP_WIDTH, True, 1),)
    *qkv, qm = _proj(x, norm_mix, w_in.astype(BF16), head_gain, segments, seq)
    km, vm = _mem_kv(mem2d, norm_mem, w_mem_kv, qk_norm[3])
    batch = x.shape[0] // seq
    attn = []
    for g, dilation in enumerate(DILATIONS):
        q, k, v = (a.reshape(batch, dilation, seq // dilation, GROUP_WIDTH) for a in qkv[g::n_grp])
        attn.append(_dil_attn(q, k, v))
    return _dil_out(x, attn, qm, km, vm, w_out.astype(BF16), seq)


def _pooling_mixer(x, mem2d, seq, norm_mix, norm_mem, w_in, w_pool, pool_scale, qk_norm, w_mem_kv, w_out):
    head_gain = jnp.concatenate([jnp.ones((POOL_WIDTH,), F32), jnp.tile(qk_norm[0], HEADS_PER_GROUP)])[None, :]
    segments = ((0, POOL_WIDTH, False, 1), (POOL_WIDTH, GROUP_WIDTH, True, 1))
    u, qm = _proj(x, norm_mix, w_in.astype(BF16), head_gain, segments, seq)
    km, vm = _mem_kv(mem2d, norm_mem, w_mem_kv, qk_norm[1])
    return _pool_out(x, u, qm, km, vm, w_pool.astype(BF16), pool_scale, w_out.astype(BF16), seq)


def kernel(x, mem, l0_norm_mix, l0_norm_mem, l0_w_in, l0_qk_norm, l0_w_mem_kv, l0_w_out, l0_norm_ffn, l0_w_gu, l0_w_down, l1_norm_mix, l1_norm_mem, l1_w_in, l1_w_pool, l1_pool_scale, l1_qk_norm, l1_w_mem_kv, l1_w_out, l1_norm_ffn, l1_w_router, l1_w_gu_e, l1_w_down_e, l2_norm_mix, l2_norm_mem, l2_w_in, l2_qk_norm, l2_w_mem_kv, l2_w_out, l2_norm_ffn, l2_w_gu, l2_w_down, l3_norm_mix, l3_norm_mem, l3_w_in, l3_w_pool, l3_pool_scale, l3_qk_norm, l3_w_mem_kv, l3_w_out, l3_norm_ffn, l3_w_router, l3_w_gu_e, l3_w_down_e):
    batch, seq, d = x.shape
    h = x.reshape(batch * seq, d)
    mem2d = mem.reshape(-1, d)

    h = _dilated_mixer(h, mem2d, seq, l0_norm_mix, l0_norm_mem, l0_w_in, l0_qk_norm, l0_w_mem_kv, l0_w_out)
    h = _ffn_dense(h, l0_norm_ffn, l0_w_gu.astype(BF16), l0_w_down.astype(BF16))
    h = _pooling_mixer(h, mem2d, seq, l1_norm_mix, l1_norm_mem, l1_w_in, l1_w_pool, l1_pool_scale,
                       l1_qk_norm, l1_w_mem_kv, l1_w_out)
    h = _moe_layer(h, l1_norm_ffn, l1_w_router, l1_w_gu_e, l1_w_down_e)
    h = _dilated_mixer(h, mem2d, seq, l2_norm_mix, l2_norm_mem, l2_w_in, l2_qk_norm, l2_w_mem_kv, l2_w_out)
    h = _ffn_dense(h, l2_norm_ffn, l2_w_gu.astype(BF16), l2_w_down.astype(BF16))
    h = _pooling_mixer(h, mem2d, seq, l3_norm_mix, l3_norm_mem, l3_w_in, l3_w_pool, l3_pool_scale,
                       l3_qk_norm, l3_w_mem_kv, l3_w_out)
    h = _moe_layer(h, l3_norm_ffn, l3_w_router, l3_w_gu_e, l3_w_down_e)
    return h.reshape(batch, seq, d)
```

```python
import functools

import jax
import jax.numpy as jnp
from jax import lax
from jax.experimental import pallas as pl
from jax.experimental.pallas import tpu as pltpu

F32 = jnp.float32
BF16 = jnp.bfloat16
U32 = jnp.uint32
I32 = jnp.int32

HEAD_DIM = 128
HEADS_PER_GROUP = 4
GROUP_WIDTH = HEADS_PER_GROUP * HEAD_DIM
DILATIONS = (1, 4, 16)
ATTN_BLOCK = 128
N_MEM = 256
POOL_WINDOWS = (2, 4, 8, 16)
POOL_GROUP = 384
POOL_WIDTH = 4 * POOL_GROUP
POOL_HALO = 16
D_FF = 7168
N_EXPERTS = 8
EPS = 1e-6
SCORE_SCALE = HEAD_DIM ** -0.5

LANES = 128
SUBLANES = 8
V7X_VMEM_BYTES = 64 * 1024 * 1024

PROJ_TM = 512
ATTN_QB = 512
OUT_TM = 512
FFN_TM, FFN_TF = 1024, 512
MOE_TM, MOE_SUB = 1024, 512
MOE_TF = 512
ROUTE_TM = 512
ROW_TM = 1024
ROW_UNROLL = 8


def _vmem_limit(nbytes):
    return int(min(nbytes * 5 // 4 + (6 << 20), V7X_VMEM_BYTES - (4 << 20)))


def _rms(x, eps=EPS):
    return x * lax.rsqrt(jnp.mean(x * x, axis=-1, keepdims=True) + eps)


def _resident(shape):
    return pl.BlockSpec(shape, lambda *_: (0,) * len(shape), pipeline_mode=pl.Buffered(1))


def _proj_kernel(x_ref, g_ref, w_ref, hg_ref, *refs, segments):
    outs, (hn_ref, stage_ref) = refs[:len(segments)], refs[len(segments):]
    tm = x_ref.shape[0]
    hn_ref[...] = (_rms(x_ref[...]) * g_ref[...]).astype(BF16)
    for (col0, width, norm, dilation), o_ref in zip(segments, outs):
        for sub in range(width // GROUP_WIDTH):
            c0 = col0 + sub * GROUP_WIDTH
            acc = jnp.dot(hn_ref[...], w_ref[:, c0:c0 + GROUP_WIDTH], preferred_element_type=F32)
            for h in range(HEADS_PER_GROUP):
                cols = slice(h * HEAD_DIM, (h + 1) * HEAD_DIM)
                val = acc[:, cols]
                if norm:
                    val = _rms(val) * hg_ref[:, c0 + h * HEAD_DIM:c0 + (h + 1) * HEAD_DIM]
                if dilation == 1:
                    o_ref[:, sub * GROUP_WIDTH + h * HEAD_DIM:sub * GROUP_WIDTH + (h + 1) * HEAD_DIM] = val.astype(BF16)
                else:
                    stage_ref[h] = val
                    for r in range(dilation):
                        o_ref[r, :, cols] = stage_ref[h, pl.ds(r, tm // dilation, stride=dilation), :].astype(BF16)


def _proj(x, gain, w_bf16, head_gain, segments, seq, tm=PROJ_TM):
    t, d = x.shape
    tm = min(tm, t)
    tiles_per_seq = max(seq // tm, 1)
    out_specs, out_shapes = [], []
    for _, width, _, dilation in segments:
        if dilation == 1:
            out_specs.append(pl.BlockSpec((tm, width), lambda i: (i, 0)))
            out_shapes.append(jax.ShapeDtypeStruct((t, width), BF16))
        else:
            out_specs.append(pl.BlockSpec((None, dilation, tm // dilation, width),
                                          lambda i: (i // tiles_per_seq, 0, i % tiles_per_seq, 0)))
            out_shapes.append(jax.ShapeDtypeStruct((t // seq, dilation, seq // dilation, width), BF16))
    n_out_cols = sum(s[1] for s in segments)
    est = (2 * tm * d * 4 + tm * d * 2 + w_bf16.size * 2 + 4 * tm * n_out_cols + tm * GROUP_WIDTH * 4
           + 4 * tm * GROUP_WIDTH * 4)
    return pl.pallas_call(
        functools.partial(_proj_kernel, segments=tuple(segments)),
        grid=(t // tm,),
        in_specs=[
            pl.BlockSpec((tm, d), lambda i: (i, 0)),
            _resident((1, d)),
            _resident(w_bf16.shape),
            _resident(head_gain.shape),
        ],
        out_specs=out_specs,
        out_shape=out_shapes,
        scratch_shapes=[pltpu.VMEM((tm, d), BF16), pltpu.VMEM((HEADS_PER_GROUP, tm, HEAD_DIM), F32)],
        compiler_params=pltpu.CompilerParams(
            dimension_semantics=("arbitrary",), vmem_limit_bytes=_vmem_limit(est)),
        name="proj",
    )(x, gain.reshape(1, d), w_bf16, head_gain)


def _dil_attn_kernel(q_ref, kp_ref, k_ref, vp_ref, v_ref, o_ref, l_ref, *, qb):
    has_prev = pl.program_id(2) > 0
    qi = lax.broadcasted_iota(I32, (ATTN_BLOCK, 2 * ATTN_BLOCK), 0)
    kj = lax.broadcasted_iota(I32, (ATTN_BLOCK, 2 * ATTN_BLOCK), 1)
    in_prev = kj < ATTN_BLOCK
    kk = jnp.where(in_prev, kj, kj - ATTN_BLOCK)
    band = jnp.logical_and(kk >= jnp.where(in_prev, qi, 0), kk <= jnp.where(in_prev, ATTN_BLOCK, qi))
    band_first = jnp.logical_and(band, jnp.logical_or(has_prev, jnp.logical_not(in_prev)))
    for jb in range(qb // ATTN_BLOCK):
        rows = slice(jb * ATTN_BLOCK, (jb + 1) * ATTN_BLOCK)
        for h in range(HEADS_PER_GROUP):
            cols = slice(h * HEAD_DIM, (h + 1) * HEAD_DIM)
            q = q_ref[rows, cols]
            if jb == 0:
                keys = jnp.concatenate([kp_ref[:, cols], k_ref[rows, cols]], axis=0)
                vals = jnp.concatenate([vp_ref[:, cols], v_ref[rows, cols]], axis=0)
                mask = band_first
            else:
                both = slice((jb - 1) * ATTN_BLOCK, (jb + 1) * ATTN_BLOCK)
                keys = k_ref[both, cols]
                vals = v_ref[both, cols]
                mask = band
            s = lax.dot_general(q, keys, (((1,), (1,)), ((), ())), preferred_element_type=F32)
            s = jnp.where(mask, s * SCORE_SCALE, -jnp.inf)
            m = jnp.max(s, axis=-1, keepdims=True)
            p = jnp.exp(s - m)
            den = jnp.sum(p, axis=-1, keepdims=True)
            o = jnp.dot(p.astype(BF16), vals, preferred_element_type=F32)
            o_ref[rows, cols] = o / den
            l_ref[rows, cols] = jnp.broadcast_to(m + jnp.log(den), (ATTN_BLOCK, HEAD_DIM))


def _dil_attn(q, k, v, qb=ATTN_QB):
    batch, dilation, length, width = q.shape
    qb = min(qb, length)
    bpq = qb // ATTN_BLOCK
    cur = pl.BlockSpec((None, None, qb, width), lambda b, r, i: (b, r, i, 0))
    prev = pl.BlockSpec((None, None, ATTN_BLOCK, width), lambda b, r, i: (b, r, jnp.maximum(i * bpq - 1, 0), 0))
    out_sds = jax.ShapeDtypeStruct(q.shape, F32)
    est = 2 * (3 * qb + 2 * ATTN_BLOCK) * width * 2 + 2 * 2 * qb * width * 4
    return pl.pallas_call(
        functools.partial(_dil_attn_kernel, qb=qb),
        grid=(batch, dilation, length // qb),
        in_specs=[cur, prev, cur, prev, cur],
        out_specs=[cur, cur],
        out_shape=[out_sds, out_sds],
        compiler_params=pltpu.CompilerParams(
            dimension_semantics=("arbitrary", "arbitrary", "arbitrary"),
            vmem_limit_bytes=_vmem_limit(est)),
        name="dil_attn",
    )(q, k, k, v, v)


def _mem_attention(qm_ref, km_ref, vm_ref, cat_ref, col0):
    for h in range(HEADS_PER_GROUP):
        cols = slice(h * HEAD_DIM, (h + 1) * HEAD_DIM)
        s = lax.dot_general(qm_ref[:, cols], km_ref[:, cols], (((1,), (1,)), ((), ())),
                            preferred_element_type=F32) * SCORE_SCALE
        m = jnp.max(s, axis=-1, keepdims=True)
        p = jnp.exp(s - m)
        den = jnp.sum(p, axis=-1, keepdims=True)
        o = jnp.dot(p.astype(BF16), vm_ref[:, cols], preferred_element_type=F32)
        cat_ref[:, col0 + h * HEAD_DIM:col0 + (h + 1) * HEAD_DIM] = (o / den).astype(BF16)


def _to_token_order(src_ref, dst_ref):
    dilation, per_res, _ = src_ref.shape
    for h in range(HEADS_PER_GROUP):
        for r in range(dilation):
            dst_ref[h, pl.ds(r, per_res, stride=dilation), :] = src_ref[r, :, h * HEAD_DIM:(h + 1) * HEAD_DIM]


def _dil_out_kernel(o0_ref, l0_ref, o1_ref, l1_ref, o2_ref, l2_ref, qm_ref, km_ref, vm_ref,
                    x_ref, w_ref, y_ref, o1s, l1s, o2s, l2s, cat_ref):
    for src, dst in ((o1_ref, o1s), (l1_ref, l1s), (o2_ref, o2s), (l2_ref, l2s)):
        _to_token_order(src, dst)
    for h in range(HEADS_PER_GROUP):
        cols = slice(h * HEAD_DIM, (h + 1) * HEAD_DIM)
        l0, l1, l2 = l0_ref[:, cols], l1s[h], l2s[h]
        mx = jnp.maximum(jnp.maximum(l0, l1), l2)
        e0, e1, e2 = jnp.exp(l0 - mx), jnp.exp(l1 - mx), jnp.exp(l2 - mx)
        den = e0 + e1 + e2
        dil = (e0 / den) * o0_ref[:, cols] + (e1 / den) * o1s[h] + (e2 / den) * o2s[h]
        cat_ref[:, cols] = dil.astype(BF16)
    _mem_attention(qm_ref, km_ref, vm_ref, cat_ref, GROUP_WIDTH)
    y_ref[...] = x_ref[...] + jnp.dot(cat_ref[...], w_ref[...], preferred_element_type=F32)


def _dil_out(x, attn, qm, km, vm, w_out_bf16, seq, tm=OUT_TM):
    t, d = x.shape
    tiles_per_seq = seq // tm

    def grp(dilation):
        return pl.BlockSpec((None, dilation, tm // dilation, GROUP_WIDTH),
                            lambda i: (i // tiles_per_seq, 0, i % tiles_per_seq, 0))

    tok = pl.BlockSpec((tm, GROUP_WIDTH), lambda i: (i, 0))
    mem = pl.BlockSpec((None, N_MEM, GROUP_WIDTH), lambda i: (i // tiles_per_seq, 0, 0))
    (o0, l0), (o1, l1), (o2, l2) = attn
    stage = pltpu.VMEM((HEADS_PER_GROUP, tm, HEAD_DIM), F32)
    est = (2 * 6 * tm * GROUP_WIDTH * 4 + 4 * tm * GROUP_WIDTH * 4 + 2 * tm * GROUP_WIDTH * 2
           + 4 * N_MEM * GROUP_WIDTH * 2 + 4 * tm * d * 4 + w_out_bf16.size * 2 + 3 * tm * d * 4)
    return pl.pallas_call(
        _dil_out_kernel,
        grid=(t // tm,),
        in_specs=[tok, tok, grp(DILATIONS[1]), grp(DILATIONS[1]), grp(DILATIONS[2]), grp(DILATIONS[2]),
                  tok, mem, mem, pl.BlockSpec((tm, d), lambda i: (i, 0)), _resident(w_out_bf16.shape)],
        out_specs=pl.BlockSpec((tm, d), lambda i: (i, 0)),
        out_shape=jax.ShapeDtypeStruct((t, d), F32),
        scratch_shapes=[stage, stage, stage, stage, pltpu.VMEM((tm, w_out_bf16.shape[0]), BF16)],
        compiler_params=pltpu.CompilerParams(
            dimension_semantics=("arbitrary",), vmem_limit_bytes=_vmem_limit(est)),
        name="dil_out",
    )(o0.reshape(t, GROUP_WIDTH), l0.reshape(t, GROUP_WIDTH), o1, l1, o2, l2, qm, km, vm, x, w_out_bf16)


def _pool_out_kernel(u_ref, halo_ref, qm_ref, km_ref, vm_ref, x_ref, wp_ref, ps_ref, w_ref, y_ref, cat_ref,
                     *, tiles_per_seq):
    tm = u_ref.shape[0]
    tile_in_seq = pl.program_id(0) % tiles_per_seq
    halo = jnp.where(tile_in_seq > 0, halo_ref[...].astype(F32), 0.0)
    full = jnp.concatenate([halo, u_ref[...].astype(F32)], axis=0)
    pos = tile_in_seq * tm + lax.broadcasted_iota(I32, (tm, 1), 0)
    acc = full
    for g, window in enumerate(POOL_WINDOWS):
        lo = g * POOL_GROUP
        acc = acc[:, (POOL_GROUP if g else 0):]
        acc = acc + pltpu.roll(acc, window // 2, 0)
        count = jnp.minimum(pos + 1, window).astype(F32)
        pooled = acc[POOL_HALO:, :POOL_GROUP] / count - full[POOL_HALO:, lo:lo + POOL_GROUP]
        po = jnp.dot(pooled.astype(BF16), wp_ref[g], preferred_element_type=F32)
        cat_ref[:, lo:lo + POOL_GROUP] = (po * ps_ref[:, lo:lo + POOL_GROUP]).astype(BF16)
    _mem_attention(qm_ref, km_ref, vm_ref, cat_ref, POOL_WIDTH)
    y_ref[...] = x_ref[...] + jnp.dot(cat_ref[...], w_ref[...], preferred_element_type=F32)


def _pool_out(x, u, qm, km, vm, w_pool_bf16, pool_scale, w_out_bf16, seq, tm=OUT_TM):
    t, d = x.shape
    tiles_per_seq = seq // tm
    halo_blocks = tm // POOL_HALO
    mem = pl.BlockSpec((None, N_MEM, GROUP_WIDTH), lambda i: (i // tiles_per_seq, 0, 0))
    est = (2 * tm * d * 2 + 4 * N_MEM * GROUP_WIDTH * 2 + 4 * tm * d * 4 + w_out_bf16.size * 2
           + w_pool_bf16.size * 2 + 8 * tm * POOL_WIDTH * 4)
    return pl.pallas_call(
        functools.partial(_pool_out_kernel, tiles_per_seq=tiles_per_seq),
        grid=(t // tm,),
        in_specs=[
            pl.BlockSpec((tm, POOL_WIDTH), lambda i: (i, 0)),
            pl.BlockSpec((POOL_HALO, POOL_WIDTH), lambda i: (jnp.maximum(i * halo_blocks - 1, 0), 0)),
            pl.BlockSpec((tm, GROUP_WIDTH), lambda i: (i, 0)),
            mem, mem,
            pl.BlockSpec((tm, d), lambda i: (i, 0)),
            _resident(w_pool_bf16.shape),
            _resident((1, POOL_WIDTH)),
            _resident(w_out_bf16.shape),
        ],
        out_specs=pl.BlockSpec((tm, d), lambda i: (i, 0)),
        out_shape=jax.ShapeDtypeStruct((t, d), F32),
        scratch_shapes=[pltpu.VMEM((tm, w_out_bf16.shape[0]), BF16)],
        compiler_params=pltpu.CompilerParams(
            dimension_semantics=("arbitrary",), vmem_limit_bytes=_vmem_limit(est)),
        name="pool_out",
    )(u, u, qm, km, vm, x, w_pool_bf16, pool_scale.reshape(1, POOL_WIDTH), w_out_bf16)


def _swiglu_step(hn, wg_ref, wu_ref, wd_ref):
    gate = jnp.dot(hn, wg_ref[...].astype(BF16), preferred_element_type=F32)
    up = jnp.dot(hn, wu_ref[...].astype(BF16), preferred_element_type=F32)
    act = (gate * (1.0 / (1.0 + jnp.exp(-gate))) * up).astype(BF16)
    return jnp.dot(act, wd_ref[...].astype(BF16), preferred_element_type=F32)


def _ffn_dense_kernel(x_ref, g_ref, wg_ref, wu_ref, wd_ref, y_ref, hn_ref):
    @pl.when(pl.program_id(1) == 0)
    def _():
        x = x_ref[...]
        hn_ref[...] = (_rms(x) * g_ref[...]).astype(BF16)
        y_ref[...] = x

    y_ref[...] += _swiglu_step(hn_ref[...], wg_ref, wu_ref, wd_ref)


def _ffn_dense(x, gain, w_gu, w_down, tm=FFN_TM, tf=FFN_TF):
    t, d = x.shape
    nf = D_FF // tf
    wbytes = w_gu.dtype.itemsize
    est = 4 * tm * d * 4 + tm * d * 2 + 2 * 3 * d * tf * wbytes + 3 * tm * tf * 4
    return pl.pallas_call(
        _ffn_dense_kernel,
        grid=(t // tm, nf),
        in_specs=[
            pl.BlockSpec((tm, d), lambda i, f: (i, 0)),
            _resident((1, d)),
            pl.BlockSpec((d, tf), lambda i, f: (0, f)),
            pl.BlockSpec((d, tf), lambda i, f: (0, f + nf)),
            pl.BlockSpec((tf, d), lambda i, f: (f, 0)),
        ],
        out_specs=pl.BlockSpec((tm, d), lambda i, f: (i, 0)),
        out_shape=jax.ShapeDtypeStruct((t, d), F32),
        scratch_shapes=[pltpu.VMEM((tm, d), BF16)],
        compiler_params=pltpu.CompilerParams(
            dimension_semantics=("arbitrary", "arbitrary"), vmem_limit_bytes=_vmem_limit(est)),
        name="ffn_dense",
    )(x, gain.reshape(1, d), w_gu, w_gu, w_down)


def _route_kernel(x_ref, g_ref, whi_ref, wlo_ref, tri_ref, hp_ref, info_ref, cnt_ref, carry_ref):
    i = pl.program_id(0)

    @pl.when(i == 0)
    def _():
        carry_ref[...] = jnp.zeros_like(carry_ref)

    hn = _rms(x_ref[...]) * g_ref[...]
    hi = hn.astype(BF16)
    hi32 = hi.astype(F32)
    lo = (hn - hi32).astype(BF16)
    logits = (jnp.dot(hi, whi_ref[...], preferred_element_type=F32)
              + jnp.dot(hi, wlo_ref[...], preferred_element_type=F32)
              + jnp.dot(lo, whi_ref[...], preferred_element_type=F32))
    tm = logits.shape[0]
    lane = lax.broadcasted_iota(I32, (tm, LANES), 1).astype(F32)
    lg = jnp.where(lane < N_EXPERTS, logits, -jnp.inf)
    m1 = jnp.max(lg, axis=-1, keepdims=True)
    i1 = jnp.min(jnp.where(lg == m1, lane, float(LANES)), axis=-1, keepdims=True)
    lg2 = jnp.where(lane == i1, -jnp.inf, lg)
    m2 = jnp.max(lg2, axis=-1, keepdims=True)
    i2 = jnp.min(jnp.where(lg2 == m2, lane, float(LANES)), axis=-1, keepdims=True)
    e = jnp.exp(m2 - m1)
    g1 = 1.0 / (1.0 + e)
    g2 = e / (1.0 + e)
    sel1 = lane == i1
    sel2 = lane == i2
    onehot = jnp.where(jnp.logical_or(sel1, sel2), 1.0, 0.0)
    before = jnp.dot(tri_ref[...], onehot.astype(BF16), preferred_element_type=F32) + carry_ref[0:1, :]
    r1 = jnp.sum(jnp.where(sel1, before, 0.0), axis=-1, keepdims=True)
    r2 = jnp.sum(jnp.where(sel2, before, 0.0), axis=-1, keepdims=True)
    carry_ref[0:1, :] = carry_ref[0:1, :] + jnp.sum(onehot, axis=0, keepdims=True)
    cnt_ref[...] = carry_ref[...]

    info = jnp.where(lane == 0, i1, 0.0)
    info = jnp.where(lane == 1, i2, info)
    info = jnp.where(lane == 2, r1, info)
    info = jnp.where(lane == 3, r2, info)
    info = jnp.where(lane == 4, g1, info)
    info = jnp.where(lane == 5, g2, info)
    info_ref[...] = info

    hp_ref[...] = _pack_halves(hi32)


def _route(x, gain, w_router, tm=ROUTE_TM):
    t, d = x.shape
    wpad = jnp.zeros((d, LANES), F32).at[:, :N_EXPERTS].set(w_router)
    whi = wpad.astype(BF16)
    wlo = (wpad - whi.astype(F32)).astype(BF16)
    tri = (lax.broadcasted_iota(I32, (tm, tm), 0) > lax.broadcasted_iota(I32, (tm, tm), 1)).astype(BF16)
    est = 4 * tm * d * 4 + 2 * tm * d * 2 + 2 * d * LANES * 2 + tm * tm * 2 + 8 * tm * LANES * 4
    return pl.pallas_call(
        _route_kernel,
        grid=(t // tm,),
        in_specs=[
            pl.BlockSpec((tm, d), lambda i: (i, 0)),
            _resident((1, d)),
            _resident((d, LANES)),
            _resident((d, LANES)),
            _resident((tm, tm)),
        ],
        out_specs=[
            pl.BlockSpec((tm, d // 2), lambda i: (i, 0)),
            pl.BlockSpec((tm, LANES), lambda i: (i, 0)),
            pl.BlockSpec((SUBLANES, LANES), lambda i: (0, 0)),
        ],
        out_shape=[
            jax.ShapeDtypeStruct((t, d // 2), U32),
            jax.ShapeDtypeStruct((t, LANES), F32),
            jax.ShapeDtypeStruct((SUBLANES, LANES), F32),
        ],
        scratch_shapes=[pltpu.VMEM((SUBLANES, LANES), F32)],
        compiler_params=pltpu.CompilerParams(
            dimension_semantics=("arbitrary",), vmem_limit_bytes=_vmem_limit(est)),
        name="route",
    )(x, gain.reshape(1, d), whi, wlo, tri)


def _pack_halves(x):
    bits = pltpu.bitcast(x.astype(BF16).astype(F32), U32)
    half = bits.shape[1] // 2
    return (bits[:, :half] & jnp.uint32(0xFFFF0000)) | (bits[:, half:] >> 16)


def _unpack_halves(words):
    return pltpu.bitcast(words & jnp.uint32(0xFFFF0000), F32), pltpu.bitcast(words << 16, F32)


def _for_each_row(tm, fn):
    def body(c, carry):
        for u in range(ROW_UNROLL):
            fn(c * ROW_UNROLL + u)
        return carry

    lax.fori_loop(0, tm // ROW_UNROLL, body, 0)


def _scatter_kernel(pos_ref, hp_ref, xs_hbm_in, xs_hbm, sem):
    del xs_hbm_in
    tm = hp_ref.shape[0]

    def row_copy(r, k):
        return pltpu.make_async_copy(hp_ref.at[pl.ds(r, 1)], xs_hbm.at[pl.ds(pos_ref[0, 2 * r + k], 1)], sem)

    def start(r):
        row_copy(r, 0).start(priority=0)
        row_copy(r, 1).start(priority=1)

    def wait(r):
        row_copy(r, 0).wait()
        row_copy(r, 1).wait()

    _for_each_row(tm, start)
    _for_each_row(tm, wait)


def _scatter_rows(hp, pos, n_slots, tm=ROW_TM):
    t = hp.shape[0]
    pos3 = pos.reshape(t // tm, 1, 2 * tm)
    return pl.pallas_call(
        _scatter_kernel,
        grid=(t // tm,),
        in_specs=[
            pl.BlockSpec((None, 1, 2 * tm), lambda i: (i, 0, 0), memory_space=pltpu.SMEM),
            pl.BlockSpec((tm, hp.shape[1]), lambda i: (i, 0)),
            pl.BlockSpec(memory_space=pl.ANY),
        ],
        out_specs=pl.BlockSpec(memory_space=pl.ANY),
        out_shape=jax.ShapeDtypeStruct((n_slots, hp.shape[1]), U32),
        scratch_shapes=[pltpu.SemaphoreType.DMA(())],
        input_output_aliases={2: 0},
        compiler_params=pltpu.CompilerParams(dimension_semantics=("arbitrary",)),
        name="scatter_rows",
    )(pos3, hp, jnp.zeros((n_slots, hp.shape[1]), U32))


def _combine_kernel(pos_ref, x_ref, info_ref, ys_hbm, y_ref, buf_ref, sem):
    tm, d = x_ref.shape

    def row_copy(r, k):
        return pltpu.make_async_copy(ys_hbm.at[pl.ds(pos_ref[0, 2 * r + k], 1)], buf_ref.at[k, pl.ds(r, 1)], sem)

    def start(r):
        row_copy(r, 0).start(priority=0)
        row_copy(r, 1).start(priority=1)

    def wait(r):
        row_copy(r, 0).wait()
        row_copy(r, 1).wait()

    _for_each_row(tm, start)
    _for_each_row(tm, wait)
    info = info_ref[...]
    g1, g2 = info[:, 4:5], info[:, 5:6]
    a_hi, a_lo = _unpack_halves(buf_ref[0])
    b_hi, b_lo = _unpack_halves(buf_ref[1])
    y_ref[:, :d // 2] = x_ref[:, :d // 2] + g1 * a_hi + g2 * b_hi
    y_ref[:, d // 2:] = x_ref[:, d // 2:] + g1 * a_lo + g2 * b_lo


def _combine(x, info, ys, pos, tm=ROW_TM):
    t, d = x.shape
    pos3 = pos.reshape(t // tm, 1, 2 * tm)
    est = 4 * tm * d * 4 + 2 * tm * LANES * 4 + 2 * tm * d * 2 + 2 * tm * d * 4
    return pl.pallas_call(
        _combine_kernel,
        grid=(t // tm,),
        in_specs=[
            pl.BlockSpec((None, 1, 2 * tm), lambda i: (i, 0, 0), memory_space=pltpu.SMEM),
            pl.BlockSpec((tm, d), lambda i: (i, 0)),
            pl.BlockSpec((tm, LANES), lambda i: (i, 0)),
            pl.BlockSpec(memory_space=pl.ANY),
        ],
        out_specs=pl.BlockSpec((tm, d), lambda i: (i, 0)),
        out_shape=jax.ShapeDtypeStruct((t, d), F32),
        scratch_shapes=[pltpu.VMEM((2, tm, ys.shape[1]), U32), pltpu.SemaphoreType.DMA(())],
        compiler_params=pltpu.CompilerParams(
            dimension_semantics=("arbitrary",), vmem_limit_bytes=_vmem_limit(est)),
        name="combine",
    )(pos3, x, info, ys)


def _ffn_moe_kernel(te_ref, tr_ref, xs_ref, wgu_hbm, wdn_hbm, y_ref, hn_ref, acc_ref,
                    wg_buf, wu_buf, wd_buf, sem):
    i, f = pl.program_id(0), pl.program_id(1)
    n_tiles, nf = pl.num_programs(0), pl.num_programs(1)
    rows = tr_ref[i]
    tm, d = acc_ref.shape
    tf = wg_buf.shape[2]
    slot = (i * nf + f) % 2

    def weight_copies(tile, chunk, to_slot):
        e = te_ref[tile]
        c0 = pl.multiple_of(chunk * tf, tf)
        half_tf = tf // 2
        return (
            pltpu.make_async_copy(wgu_hbm.at[e, :, pl.ds(c0, tf)], wg_buf.at[to_slot], sem.at[0, to_slot]),
            pltpu.make_async_copy(wgu_hbm.at[e, :, pl.ds(c0 + nf * tf, tf)], wu_buf.at[to_slot], sem.at[1, to_slot]),
            pltpu.make_async_copy(wdn_hbm.at[e, pl.ds(c0, half_tf), :],
                                  wd_buf.at[to_slot, pl.ds(0, half_tf)], sem.at[2, to_slot]),
            pltpu.make_async_copy(wdn_hbm.at[e, pl.ds(c0 + half_tf, half_tf), :],
                                  wd_buf.at[to_slot, pl.ds(half_tf, half_tf)], sem.at[3, to_slot]),
        )

    def start_weights(tile, chunk, to_slot):
        for n, copy in enumerate(weight_copies(tile, chunk, to_slot)):
            copy.start(priority=n % 2)

    @pl.when(jnp.logical_and(i == 0, f == 0))
    def _():
        start_weights(0, 0, 0)

    last_chunk = f == nf - 1
    next_tile = jnp.minimum(jnp.where(last_chunk, i + 1, i), n_tiles - 1)
    next_chunk = jnp.where(last_chunk, 0, f + 1)
    has_next = jnp.logical_and(jnp.logical_not(jnp.logical_and(last_chunk, i == n_tiles - 1)),
                               tr_ref[next_tile] > 0)

    @pl.when(has_next)
    def _():
        start_weights(next_tile, next_chunk, 1 - slot)

    @pl.when(rows > 0)
    def _():
        for copy in weight_copies(i, f, slot):
            copy.wait()

    wg_ref, wu_ref, wd_ref = wg_buf.at[slot], wu_buf.at[slot], wd_buf.at[slot]

    @pl.when(f == 0)
    def _():
        acc_ref[...] = jnp.zeros_like(acc_ref)

    @pl.when(jnp.logical_and(rows > 0, f == 0))
    def _():
        hi, lo = _unpack_halves(xs_ref[...])
        hn_ref[:, :d // 2] = hi.astype(BF16)
        hn_ref[:, d // 2:] = lo.astype(BF16)

    for n_sub in range(1, tm // MOE_SUB + 1):
        live = n_sub * MOE_SUB
        in_range = jnp.logical_and(rows > live - MOE_SUB, rows <= live)

        @pl.when(in_range)
        def _():
            acc_ref[0:live, :] += _swiglu_step(hn_ref[0:live, :], wg_ref, wu_ref, wd_ref)

    @pl.when(f == pl.num_programs(1) - 1)
    def _():
        y_ref[...] = _pack_halves(acc_ref[...])


def _ffn_moe(xs, tile_expert, tile_rows, w_gu, w_down, tm=MOE_TM, tf=MOE_TF):
    n_slots, half = xs.shape
    d = 2 * half
    nf = D_FF // tf
    n_tiles = n_slots // tm
    wdtype = w_gu.dtype
    est = 4 * tm * half * 4 + tm * d * 2 + tm * d * 4 + 2 * 3 * d * tf * wdtype.itemsize + 3 * d * tf * 2 + 3 * tm * tf * 4
    return pl.pallas_call(
        _ffn_moe_kernel,
        grid_spec=pltpu.PrefetchScalarGridSpec(
            num_scalar_prefetch=2,
            grid=(n_tiles, nf),
            in_specs=[
                pl.BlockSpec((tm, half), lambda i, f, te, tr: (i, 0)),
                pl.BlockSpec(memory_space=pl.ANY),
                pl.BlockSpec(memory_space=pl.ANY),
            ],
            out_specs=pl.BlockSpec((tm, half), lambda i, f, te, tr: (i, 0)),
            scratch_shapes=[
                pltpu.VMEM((tm, d), BF16), pltpu.VMEM((tm, d), F32),
                pltpu.VMEM((2, d, tf), wdtype), pltpu.VMEM((2, d, tf), wdtype), pltpu.VMEM((2, tf, d), wdtype),
                pltpu.SemaphoreType.DMA((4, 2)),
            ],
        ),
        out_shape=jax.ShapeDtypeStruct((n_slots, half), U32),
        compiler_params=pltpu.CompilerParams(
            dimension_semantics=("arbitrary", "arbitrary"), vmem_limit_bytes=_vmem_limit(est)),
        name="ffn_moe",
    )(tile_expert, tile_rows, xs, w_gu, w_down)


def _moe_layer(x, gain, w_router, w_gu, w_down, tm=MOE_TM):
    t, _ = x.shape
    hp, info, cnt = _route(x, gain, w_router)
    counts = cnt[0, :N_EXPERTS].astype(I32)
    tiles_per_expert = (counts + tm - 1) // tm
    tile_end = jnp.cumsum(tiles_per_expert)
    group_start = (tile_end - tiles_per_expert) * tm
    n_tiles = (2 * t + N_EXPERTS * (tm - 1)) // tm
    tile_ids = jnp.arange(n_tiles, dtype=I32)
    tile_expert = jnp.minimum(jnp.sum(tile_ids[:, None] >= tile_end[None, :], axis=1), N_EXPERTS - 1).astype(I32)
    last_used_expert = tile_expert[jnp.maximum(tile_end[-1] - 1, 0)]
    used = tile_ids < tile_end[-1]
    tile_expert = jnp.where(used, tile_expert, last_used_expert)
    rows_left = counts[tile_expert] - (tile_ids * tm - group_start[tile_expert])
    tile_rows = jnp.where(used, jnp.clip(rows_left, 0, tm), 0).astype(I32)
    idx = info[:, 0:2].astype(I32)
    rank = info[:, 2:4].astype(I32)
    pos = (group_start[idx] + rank).astype(I32)

    xs = _scatter_rows(hp, pos, n_tiles * tm)
    ys = _ffn_moe(xs, tile_expert, tile_rows, w_gu, w_down)
    return _combine(x, info, ys, pos)


def _mem_kv(mem2d, norm_mem, w_mem_kv, k_gain):
    head_gain = jnp.concatenate([jnp.tile(k_gain, HEADS_PER_GROUP), jnp.ones((GROUP_WIDTH,), F32)])[None, :]
    segments = ((0, GROUP_WIDTH, True, 1), (GROUP_WIDTH, GROUP_WIDTH, False, 1))
    km, vm = _proj(mem2d, norm_mem, w_mem_kv.astype(BF16), head_gain, segments, seq=mem2d.shape[0])
    return km.reshape(-1, N_MEM, GROUP_WIDTH), vm.reshape(-1, N_MEM, GROUP_WIDTH)


def _dilated_mixer(x, mem2d, seq, norm_mix, norm_mem, w_in, qk_norm, w_mem_kv, w_out):
    n_grp = len(DILATIONS)
    head_gain = jnp.concatenate([
        jnp.tile(qk_norm[0], n_grp * HEADS_PER_GROUP), jnp.tile(qk_norm[1], n_grp * HEADS_PER_GROUP),
        jnp.ones((n_grp * GROUP_WIDTH,), F32), jnp.tile(qk_norm[2], HEADS_PER_GROUP)])[None, :]
    segments = tuple((c * GROUP_WIDTH, GROUP_WIDTH, c < 2 * n_grp, DILATIONS[c % n_grp]) for c in range(3 * n_grp))
    segments += ((3 * n_grp * GROUP_WIDTH, GROUP_WIDTH, True, 1),)
    *qkv, qm = _proj(x, norm_mix, w_in.astype(BF16), head_gain, segments, seq)
    km, vm = _mem_kv(mem2d, norm_mem, w_mem_kv, qk_norm[3])
    batch = x.shape[0] // seq
    attn = []
    for g, dilation in enumerate(DILATIONS):
        q, k, v = (a.reshape(batch, dilation, seq // dilation, GROUP_WIDTH) for a in qkv[g::n_grp])
        attn.append(_dil_attn(q, k, v))
    return _dil_out(x, attn, qm, km, vm, w_out.astype(BF16), seq)


def _pooling_mixer(x, mem2d, seq, norm_mix, norm_mem, w_in, w_pool, pool_scale, qk_norm, w_mem_kv, w_out):
    head_gain = jnp.concatenate([jnp.ones((POOL_WIDTH,), F32), jnp.tile(qk_norm[0], HEADS_PER_GROUP)])[None, :]
    segments = ((0, POOL_WIDTH, False, 1), (POOL_WIDTH, GROUP_WIDTH, True, 1))
    u, qm = _proj(x, norm_mix, w_in.astype(BF16), head_gain, segments, seq)
    km, vm = _mem_kv(mem2d, norm_mem, w_mem_kv, qk_norm[1])
    return _pool_out(x, u, qm, km, vm, w_pool.astype(BF16), pool_scale, w_out.astype(BF16), seq)


def kernel(x, mem, l0_norm_mix, l0_norm_mem, l0_w_in, l0_qk_norm, l0_w_mem_kv, l0_w_out, l0_norm_ffn, l0_w_gu, l0_w_down, l1_norm_mix, l1_norm_mem, l1_w_in, l1_w_pool, l1_pool_scale, l1_qk_norm, l1_w_mem_kv, l1_w_out, l1_norm_ffn, l1_w_router, l1_w_gu_e, l1_w_down_e, l2_norm_mix, l2_norm_mem, l2_w_in, l2_qk_norm, l2_w_mem_kv, l2_w_out, l2_norm_ffn, l2_w_gu, l2_w_down, l3_norm_mix, l3_norm_mem, l3_w_in, l3_w_pool, l3_pool_scale, l3_qk_norm, l3_w_mem_kv, l3_w_out, l3_norm_ffn, l3_w_router, l3_w_gu_e, l3_w_down_e):
    batch, seq, d = x.shape
    h = x.reshape(batch * seq, d)
    mem2d = mem.reshape(-1, d)

    h = _dilated_mixer(h, mem2d, seq, l0_norm_mix, l0_norm_mem, l0_w_in, l0_qk_norm, l0_w_mem_kv, l0_w_out)
    h = _ffn_dense(h, l0_norm_ffn, l0_w_gu.astype(BF16), l0_w_down.astype(BF16))
    h = _pooling_mixer(h, mem2d, seq, l1_norm_mix, l1_norm_mem, l1_w_in, l1_w_pool, l1_pool_scale,
                       l1_qk_norm, l1_w_mem_kv, l1_w_out)
    h = _moe_layer(h, l1_norm_ffn, l1_w_router, l1_w_gu_e, l1_w_down_e)
    h = _dilated_mixer(h, mem2d, seq, l2_norm_mix, l2_norm_mem, l2_w_in, l2_qk_norm, l2_w_mem_kv, l2_w_out)
    h = _ffn_dense(h, l2_norm_ffn, l2_w_gu.astype(BF16), l2_w_down.astype(BF16))
    h = _pooling_mixer(h, mem2d, seq, l3_norm_mix, l3_norm_mem, l3_w_in, l3_w_pool, l3_pool_scale,
                       l3_qk_norm, l3_w_mem_kv, l3_w_out)
    h = _moe_layer(h, l3_norm_ffn, l3_w_router, l3_w_gu_e, l3_w_down_e)
    return h.reshape(batch, seq, d)
```
